```python
import jax
import jax.numpy as jnp
from jax import lax
import numpy as np

D_MODEL = 4096
BATCH = 1
SEQ = 8192
DEPTH = 2

CHUNK = 64
N_MEM = 256
ML_HEADS = 8
ML_DQK = 128
ML_DV = 256
ML_WIDTH = ML_HEADS * ML_DV
GATE_SOFTCAP = 15.0
MLA_HEADS = 16
MLA_NOPE = 128
MLA_ROPE = 64
MLA_DV = 128
MLA_QK = MLA_NOPE + MLA_ROPE
MLA_WIDTH = MLA_HEADS * MLA_DV
Q_LORA = 1024
KV_LORA = 512
ROPE_THETA = 10000.0
Q_BLOCK = 128
MIX_WIDTH = ML_WIDTH + MLA_WIDTH
X_HEADS = 4
X_HEAD_DIM = 256
X_WIDTH = X_HEADS * X_HEAD_DIM
N_EXPERTS = 32
TOP_K = 4
D_EXPERT = 768
SWIGLU_LIMIT = 7.0
SWIGLU_ALPHA = 1.702
EXPERT_BLOCK = 128
DN_ALPHA = (2.0 * DEPTH) ** 0.25
DN_BETA = (8.0 * DEPTH) ** -0.25
LN_EPS = 1e-5
RMS_EPS = 1e-6
IN_SIZES = (ML_HEADS * ML_DQK, ML_HEADS * ML_DQK, ML_WIDTH, ML_WIDTH, ML_HEADS, ML_HEADS, Q_LORA, KV_LORA, MLA_ROPE)
IN_WIDTH = sum(IN_SIZES)
IN_SPLITS = tuple(sum(IN_SIZES[:i + 1]) for i in range(len(IN_SIZES) - 1))

kernel_name = 'hybrid_mlstm_mla_moe_deepnorm'


def layer_norm(x, g, b):
    xf = x.astype(jnp.float32)
    mu = jnp.mean(xf, axis=-1, keepdims=True)
    var = jnp.mean(jnp.square(xf - mu), axis=-1, keepdims=True)
    return ((xf - mu) * lax.rsqrt(var + LN_EPS) * g + b).astype(x.dtype)


def rms_norm(x, g):
    xf = x.astype(jnp.float32)
    y = xf * lax.rsqrt(jnp.mean(jnp.square(xf), axis=-1, keepdims=True) + RMS_EPS)
    return (y * g).astype(x.dtype)


def softcap(x):
    return GATE_SOFTCAP * jnp.tanh(x / GATE_SOFTCAP)


def rope(x, pos):
    half = x.shape[-1] // 2
    inv = ROPE_THETA ** (-jnp.arange(half, dtype=jnp.float32) / half)
    ang = pos.astype(jnp.float32)[:, :, None, None] * inv
    cos, sin = jnp.cos(ang), jnp.sin(ang)
    xf = x.astype(jnp.float32)
    x1, x2 = xf[..., :half], xf[..., half:]
    return jnp.concatenate([x1 * cos - x2 * sin, x2 * cos + x1 * sin], axis=-1).astype(x.dtype)


def mlstm_chunkwise(q, k, v, ig, fg):
    B, S, H, dqk = q.shape
    dv = v.shape[-1]
    nc = S // CHUNK

    def to_chunks(t):
        t = t.astype(jnp.float32).reshape((B, nc, CHUNK) + t.shape[2:])
        return jnp.moveaxis(t, (1, 3), (0, 2))

    qc = to_chunks(q)
    kc = to_chunks(k) * (dqk ** -0.5)
    vc = to_chunks(v)
    lic = to_chunks(ig)
    lfc = jax.nn.log_sigmoid(to_chunks(fg))
    causal = jnp.tril(jnp.ones((CHUNK, CHUNK), dtype=bool))

    def step(carry, inp):
        c_prev, n_prev, m_prev = carry
        qt, kt, vt, li, lf = inp
        b = jnp.cumsum(lf, axis=-1)
        d_mat = jnp.where(causal, b[..., :, None] - b[..., None, :] + li[..., None, :], -jnp.inf)
        a_inter = b + m_prev[..., None]
        m_t = jnp.maximum(a_inter, jnp.max(d_mat, axis=-1))
        w_inter = jnp.exp(a_inter - m_t)
        s_qk = jnp.einsum('bhtd,bhsd->bhts', qt, kt) * jnp.exp(d_mat - m_t[..., None])
        num = w_inter[..., None] * jnp.einsum('bhvd,bhtd->bhtv', c_prev, qt) + jnp.einsum('bhts,bhsv->bhtv', s_qk, vt)
        den = w_inter * jnp.einsum('bhd,bhtd->bht', n_prev, qt) + jnp.sum(s_qk, axis=-1)
        h = num / jnp.maximum(jnp.abs(den), jnp.exp(-m_t))[..., None]
        m_new = m_t[..., -1]
        w_prev = jnp.exp(b[..., -1] + m_prev - m_new)
        w_s = jnp.exp(b[..., -1:] - b + li - m_new[..., None])
        c_new = w_prev[..., None, None] * c_prev + jnp.einsum('bhs,bhsv,bhsd->bhvd', w_s, vt, kt)
        n_new = w_prev[..., None] * n_prev + jnp.einsum('bhs,bhsd->bhd', w_s, kt)
        return (c_new, n_new, m_new), h

    init = (jnp.zeros((B, H, dv, dqk), jnp.float32), jnp.zeros((B, H, dqk), jnp.float32), jnp.zeros((B, H), jnp.float32))
    _, h = lax.scan(step, init, (qc, kc, vc, lic, lfc))
    return jnp.moveaxis(h, (0, 2), (1, 3)).reshape(B, S, H, dv)


def chunk_causal_attention(q, k, v):
    B, S, H, dk = q.shape
    dv = v.shape[-1]
    nqb = S // Q_BLOCK
    scale = dk ** -0.5
    qb = q.reshape(B, nqb, Q_BLOCK, H, dk).transpose(1, 0, 2, 3, 4)
    k_chunk = jnp.arange(S) // CHUNK

    def attend(args):
        q_blk, blk = args
        s = jnp.einsum('bqhd,bkhd->bhqk', q_blk, k, preferred_element_type=jnp.float32) * scale
        q_chunk = (blk * Q_BLOCK + jnp.arange(Q_BLOCK)) // CHUNK
        s = jnp.where(k_chunk[None, :] <= q_chunk[:, None], s, -jnp.inf)
        p = jax.nn.softmax(s, axis=-1)
        return jnp.einsum('bhqk,bkhd->bqhd', p.astype(v.dtype), v)

    out = lax.map(attend, (qb, jnp.arange(nqb)))
    return out.transpose(1, 0, 2, 3, 4).reshape(B, S, H, dv)


def mla_group(cq, ckv, kr, pos, g_q, w_uq, g_kv, w_ukv):
    B, S, _ = cq.shape
    q = (rms_norm(cq, g_q) @ w_uq).reshape(B, S, MLA_HEADS, MLA_QK)
    q = jnp.concatenate([q[..., :MLA_NOPE], rope(q[..., MLA_NOPE:], pos)], axis=-1)
    kv = (rms_norm(ckv, g_kv) @ w_ukv).reshape(B, S, MLA_HEADS, MLA_NOPE + MLA_DV)
    k_rope = jnp.broadcast_to(rope(kr[:, :, None, :], pos), (B, S, MLA_HEADS, MLA_ROPE))
    k = jnp.concatenate([kv[..., :MLA_NOPE], k_rope], axis=-1)
    v = kv[..., MLA_NOPE:]
    return chunk_causal_attention(q, k, v)


def hybrid_mixer(x, pos, w_in, ml_b_i, ml_b_f, ml_norm_g, mla_g_q, mla_w_uq, mla_g_kv, mla_w_ukv, w_out):
    B, S, _ = x.shape
    q, k, v, o, ig, fg, cq, ckv, kr = jnp.split(x @ w_in, IN_SPLITS, axis=-1)
    ig = softcap(ig.astype(jnp.float32) + ml_b_i)
    fg = softcap(fg.astype(jnp.float32) + ml_b_f)
    h = mlstm_chunkwise(q.reshape(B, S, ML_HEADS, ML_DQK), k.reshape(B, S, ML_HEADS, ML_DQK),
                        v.reshape(B, S, ML_HEADS, ML_DV), ig, fg)
    h = rms_norm(h, ml_norm_g.reshape(ML_HEADS, ML_DV)) * jax.nn.sigmoid(o.astype(jnp.float32)).reshape(B, S, ML_HEADS, ML_DV)
    y_ml = h.reshape(B, S, ML_WIDTH).astype(x.dtype)
    y_mla = mla_group(cq, ckv, kr, pos, mla_g_q, mla_w_uq, mla_g_kv, mla_w_ukv).reshape(B, S, MLA_WIDTH).astype(x.dtype)
    return jnp.concatenate([y_ml, y_mla], axis=-1) @ w_out


def memory_cross_attention(x, mem, g_m, b_m, w_q, w_kv, w_o):
    B, S, _ = x.shape
    M = mem.shape[1]
    mem_n = layer_norm(mem, g_m, b_m)
    q = (x @ w_q).reshape(B, S, X_HEADS, X_HEAD_DIM)
    kv = mem_n @ w_kv
    k = kv[..., :X_WIDTH].reshape(B, M, X_HEADS, X_HEAD_DIM)
    v = kv[..., X_WIDTH:].reshape(B, M, X_HEADS, X_HEAD_DIM)
    s = jnp.einsum('bqhd,bmhd->bhqm', q, k, preferred_element_type=jnp.float32) * (X_HEAD_DIM ** -0.5)
    p = jax.nn.softmax(s, axis=-1)
    o = jnp.einsum('bhqm,bmhd->bqhd', p.astype(v.dtype), v).reshape(B, S, X_WIDTH)
    return o @ w_o


def moe_ffn(x, w_router, b_router, w_gu, b_gu, w_down, b_down):
    B, S, D = x.shape
    T = B * S
    xt = x.reshape(T, D)
    logits = (xt @ w_router).astype(jnp.float32) + b_router
    top_val, top_idx = lax.top_k(logits, TOP_K)
    gates = jax.nn.softmax(top_val, axis=-1)
    A = T * TOP_K
    e_flat = top_idx.reshape(A)
    tok_flat = jnp.arange(A, dtype=jnp.int32) // TOP_K
    g_flat = gates.reshape(A)
    order = jnp.argsort(e_flat)
    e_sorted, tok_sorted, g_sorted = e_flat[order], tok_flat[order], g_flat[order]
    counts = jnp.bincount(e_flat, length=N_EXPERTS)
    starts = jnp.cumsum(counts) - counts
    padded = (counts + EXPERT_BLOCK - 1) // EXPERT_BLOCK * EXPERT_BLOCK
    pad_ends = jnp.cumsum(padded)
    pad_starts = pad_ends - padded
    dest = pad_starts[e_sorted] + jnp.arange(A) - starts[e_sorted]
    nb = -(-A // EXPERT_BLOCK) + N_EXPERTS
    P = nb * EXPERT_BLOCK
    row_tok = jnp.full((P,), T, jnp.int32).at[dest].set(tok_sorted)
    row_gate = jnp.zeros((P,), jnp.float32).at[dest].set(g_sorted)
    block_expert = jnp.clip(jnp.searchsorted(pad_ends, jnp.arange(nb) * EXPERT_BLOCK, side='right'), 0, N_EXPERTS - 1)
    x_pad = jnp.concatenate([xt, jnp.zeros((1, D), xt.dtype)], axis=0)

    def expert_block(args):
        toks, e = args
        gu = x_pad[toks] @ w_gu[e] + b_gu[e]
        gate = jnp.minimum(gu[:, :D_EXPERT], SWIGLU_LIMIT)
        up = jnp.clip(gu[:, D_EXPERT:], -SWIGLU_LIMIT, SWIGLU_LIMIT)
        h = (up + 1.0) * gate * jax.nn.sigmoid(SWIGLU_ALPHA * gate)
        return h @ w_down[e] + b_down[e]

    y_rows = lax.map(expert_block, (row_tok.reshape(nb, EXPERT_BLOCK), block_expert)).reshape(P, D)
    y = jax.ops.segment_sum(y_rows * row_gate[:, None].astype(y_rows.dtype), row_tok, num_segments=T + 1)[:T]
    return y.reshape(B, S, D)


def setup_inputs(seed: int = 0) -> dict:
    key = jax.random.key(seed)
    ks = iter(jax.random.split(key, 64))
    f32 = jnp.float32
    L, D = DEPTH, D_MODEL

    def nrm(shape, scale):
        return jax.random.normal(next(ks), shape, f32) * scale

    def gain(shape):
        return 1.0 + nrm(shape, 0.02)

    x = nrm((BATCH, SEQ, D), 1.0)
    mem = nrm((BATCH, N_MEM, D), 1.0)
    positions = jnp.arange(SEQ, dtype=jnp.int32)[None, :] + jax.random.randint(next(ks), (BATCH, 1), 0, 4096, dtype=jnp.int32)
    in_scales = (1.0, 1.0, DN_BETA, 1.0, 1.0, 1.0, 1.0, 1.0, 1.0)
    w_in = jnp.concatenate([nrm((L, D, sz), D ** -0.5 * sc) for sz, sc in zip(IN_SIZES, in_scales)], axis=-1)
    ml_b_i = nrm((L, ML_HEADS), 0.1)
    ml_b_f = 3.0 + nrm((L, ML_HEADS), 0.5)
    ml_norm_g = gain((L, ML_WIDTH))
    mla_g_q = gain((L, Q_LORA))
    mla_w_uq = nrm((L, Q_LORA, MLA_HEADS * MLA_QK), Q_LORA ** -0.5)
    mla_g_kv = gain((L, KV_LORA))
    mla_w_ukv = jnp.concatenate([nrm((L, KV_LORA, MLA_HEADS, MLA_NOPE), KV_LORA ** -0.5),
                                 nrm((L, KV_LORA, MLA_HEADS, MLA_DV), KV_LORA ** -0.5 * DN_BETA)], axis=-1).reshape(L, KV_LORA, MLA_HEADS * (MLA_NOPE + MLA_DV))
    w_out = nrm((L, MIX_WIDTH, D), MIX_WIDTH ** -0.5 * DN_BETA)
    ln1_g, ln1_b = gain((L, D)), nrm((L, D), 0.02)
    x_g_mem, x_b_mem = gain((L, D)), nrm((L, D), 0.02)
    x_w_q = nrm((L, D, X_WIDTH), D ** -0.5)
    x_w_kv = jnp.concatenate([nrm((L, D, X_WIDTH), D ** -0.5), nrm((L, D, X_WIDTH), D ** -0.5 * DN_BETA)], axis=-1)
    x_w_o = nrm((L, X_WIDTH, D), X_WIDTH ** -0.5 * DN_BETA)
    ln2_g, ln2_b = gain((L, D)), nrm((L, D), 0.02)
    w_router = nrm((L, D, N_EXPERTS), D ** -0.5)
    b_router = nrm((L, N_EXPERTS), 0.01)
    w_gu = nrm((L, N_EXPERTS, D, 2 * D_EXPERT), D ** -0.5)
    b_gu = nrm((L, N_EXPERTS, 2 * D_EXPERT), 0.01)
    w_down = nrm((L, N_EXPERTS, D_EXPERT, D), D_EXPERT ** -0.5 * DN_BETA)
    b_down = nrm((L, N_EXPERTS, D), 0.01)
    ln3_g, ln3_b = gain((L, D)), nrm((L, D), 0.02)
    return {'x': x, 'mem': mem, 'positions': positions,
            'w_in': w_in, 'ml_b_i': ml_b_i, 'ml_b_f': ml_b_f, 'ml_norm_g': ml_norm_g,
            'mla_g_q': mla_g_q, 'mla_w_uq': mla_w_uq, 'mla_g_kv': mla_g_kv, 'mla_w_ukv': mla_w_ukv,
            'w_out': w_out, 'ln1_g': ln1_g, 'ln1_b': ln1_b,
            'x_g_mem': x_g_mem, 'x_b_mem': x_b_mem, 'x_w_q': x_w_q, 'x_w_kv': x_w_kv, 'x_w_o': x_w_o,
            'ln2_g': ln2_g, 'ln2_b': ln2_b,
            'w_router': w_router, 'b_router': b_router, 'w_gu': w_gu, 'b_gu': b_gu,
            'w_down': w_down, 'b_down': b_down, 'ln3_g': ln3_g, 'ln3_b': ln3_b}


def reference(x, mem, positions, w_in, ml_b_i, ml_b_f, ml_norm_g, mla_g_q, mla_w_uq, mla_g_kv, mla_w_ukv,
              w_out, ln1_g, ln1_b, x_g_mem, x_b_mem, x_w_q, x_w_kv, x_w_o, ln2_g, ln2_b,
              w_router, b_router, w_gu, b_gu, w_down, b_down, ln3_g, ln3_b):
    for l in range(DEPTH):
        h = hybrid_mixer(x, positions, w_in[l], ml_b_i[l], ml_b_f[l], ml_norm_g[l], mla_g_q[l], mla_w_uq[l],
                         mla_g_kv[l], mla_w_ukv[l], w_out[l])
        x = layer_norm(DN_ALPHA * x + h, ln1_g[l], ln1_b[l])
        h = memory_cross_attention(x, mem, x_g_mem[l], x_b_mem[l], x_w_q[l], x_w_kv[l], x_w_o[l])
        x = layer_norm(DN_ALPHA * x + h, ln2_g[l], ln2_b[l])
        h = moe_ffn(x, w_router[l], b_router[l], w_gu[l], b_gu[l], w_down[l], b_down[l])
        x = layer_norm(DN_ALPHA * x + h, ln3_g[l], ln3_b[l])
    return x
```

```python
import functools

import jax
import jax.numpy as jnp
from jax import lax
from jax.experimental import pallas as pl
from jax.experimental.pallas import tpu as pltpu

D_MODEL = 4096
DEPTH = 2
CHUNK = 64
ML_HEADS = 8
ML_DQK = 128
ML_DV = 256
ML_WIDTH = ML_HEADS * ML_DV
GATE_SOFTCAP = 15.0
MLA_HEADS = 16
MLA_NOPE = 128
MLA_ROPE = 64
MLA_DV = 128
MLA_QK = MLA_NOPE + MLA_ROPE
MLA_WIDTH = MLA_HEADS * MLA_DV
MLA_QPAD = 256
Q_LORA = 1024
KV_LORA = 512
ROPE_THETA = 10000.0
X_HEADS = 4
X_HEAD_DIM = 256
X_WIDTH = X_HEADS * X_HEAD_DIM
N_EXPERTS = 32
TOP_K = 4
D_EXPERT = 768
SWIGLU_LIMIT = 7.0
SWIGLU_ALPHA = 1.702
DN_ALPHA = (2.0 * DEPTH) ** 0.25
LN_EPS = 1e-5
RMS_EPS = 1e-6
QKVO_WIDTH = 2 * ML_HEADS * ML_DQK + 2 * ML_WIDTH
GATES_OFF = QKVO_WIDTH
CQ_OFF = GATES_OFF + 2 * ML_HEADS
CKV_OFF = CQ_OFF + Q_LORA
KR_OFF = CKV_OFF + KV_LORA
LANES = 128
AUX_WIDTH = 1792

VMEM_LIMIT = 56 * 1024 * 1024
MOE_BLOCK = 256
ML_CHUNK = 128

BF16 = jnp.bfloat16
F32 = jnp.float32


def _params(n_axes, vmem=VMEM_LIMIT):
    return pltpu.CompilerParams(dimension_semantics=("arbitrary",) * n_axes, vmem_limit_bytes=vmem)


def _cast_weight(w_ref, wb_ref, rows=512):
    k = w_ref.shape[0]
    rows = min(rows, k)

    def body(c, carry):
        sl = pl.ds(pl.multiple_of(c * rows, rows), rows)
        wb_ref[sl, :] = w_ref[sl, :].astype(BF16)
        return carry

    lax.fori_loop(0, k // rows, body, 0)


def _mm_kernel(*refs, n_x, epilogue):
    x_refs = refs[:n_x]
    w_ref = refs[n_x]
    extra = refs[n_x + 1:-2]
    o_ref, wb_ref = refs[-2:]

    @pl.when(pl.program_id(1) == 0)
    def _():
        _cast_weight(w_ref, wb_ref)

    off = 0
    acc = None
    for x_ref in x_refs:
        kx = x_ref.shape[1]
        part = jnp.dot(x_ref[...], wb_ref[off:off + kx, :], preferred_element_type=F32)
        acc = part if acc is None else acc + part
        off += kx
    if epilogue is not None:
        acc = epilogue(acc, *extra)
    o_ref[...] = acc.astype(o_ref.dtype)


def _matmul(xs, w, layer, *, tm, tn, n_out, col_off=0, out_dtype=BF16, epilogue=None, extra=(), extra_specs=(),
            name="mm"):
    m = xs[0].shape[0]
    k = sum(x.shape[1] for x in xs)
    assert w.shape[1] == k and m % tm == 0 and n_out % tn == 0 and col_off % tn == 0
    cb = col_off // tn
    in_specs = [pl.BlockSpec((tm, x.shape[1]), lambda j, i: (i, 0)) for x in xs]
    in_specs.append(pl.BlockSpec((None, k, tn), lambda j, i: (layer, 0, j + cb)))
    in_specs.extend(extra_specs)
    return pl.pallas_call(
        functools.partial(_mm_kernel, n_x=len(xs), epilogue=epilogue),
        grid=(n_out // tn, m // tm),
        in_specs=in_specs,
        out_specs=pl.BlockSpec((tm, tn), lambda j, i: (i, j)),
        out_shape=jax.ShapeDtypeStruct((m, n_out), out_dtype),
        scratch_shapes=[pltpu.VMEM((k, tn), BF16)],
        compiler_params=_params(2),
        name=name,
    )(*xs, w, *extra)


def _ln_rows(v, g, b):
    mu = jnp.mean(v, axis=-1, keepdims=True)
    c = v - mu
    var = jnp.mean(c * c, axis=-1, keepdims=True)
    return c * lax.rsqrt(var + LN_EPS) * g + b


def _res_ln_kernel(x_ref, h_ref, g_ref, b_ref, o_ref, ob_ref):
    v = DN_ALPHA * x_ref[...] + h_ref[...].astype(F32)
    y = _ln_rows(v, g_ref[...], b_ref[...])
    o_ref[...] = y
    ob_ref[...] = y.astype(BF16)


def _res_ln(x, h, g, b, layer, *, tm=256):
    m, d = x.shape
    row = pl.BlockSpec((tm, d), lambda i: (i, 0))
    par = pl.BlockSpec((None, 1, d), lambda i: (layer, 0, 0))
    return pl.pallas_call(
        _res_ln_kernel,
        grid=(m // tm,),
        in_specs=[row, row, par, par],
        out_specs=[row, row],
        out_shape=[jax.ShapeDtypeStruct((m, d), F32), jax.ShapeDtypeStruct((m, d), BF16)],
        compiler_params=_params(1),
        name="res_ln",
    )(x, h, g.reshape(DEPTH, 1, d), b.reshape(DEPTH, 1, d))


def _ln_kernel(x_ref, g_ref, b_ref, ob_ref):
    ob_ref[...] = _ln_rows(x_ref[...], g_ref[...], b_ref[...]).astype(BF16)


def _ln_bf16(x, g, b, layer, *, tm=256):
    m, d = x.shape
    row = pl.BlockSpec((tm, d), lambda i: (i, 0))
    par = pl.BlockSpec((None, 1, d), lambda i: (layer, 0, 0))
    return pl.pallas_call(
        _ln_kernel,
        grid=(m // tm,),
        in_specs=[row, par, par],
        out_specs=row,
        out_shape=jax.ShapeDtypeStruct((m, d), BF16),
        compiler_params=_params(1),
        name="mem_ln",
    )(x, g.reshape(DEPTH, 1, d), b.reshape(DEPTH, 1, d))


def _prep_kernel(cq_ref, ckv_ref, misc_ref, pos_ref, inv_ref, sign_ref, gq_ref, gkv_ref, bias_ref,
                 cqn_ref, ckvn_ref, krope_ref, cs_ref, gates_ref):
    def rms(v, g):
        return v * lax.rsqrt(jnp.mean(v * v, axis=-1, keepdims=True) + RMS_EPS) * g

    cqn_ref[...] = rms(cq_ref[...], gq_ref[...]).astype(BF16)
    ckvn_ref[...] = rms(ckv_ref[...], gkv_ref[...]).astype(BF16)
    ang = pos_ref[...].astype(F32) * inv_ref[...]
    lane = lax.broadcasted_iota(jnp.int32, ang.shape, 1)
    cs = jnp.where(lane < MLA_ROPE, jnp.cos(ang), jnp.sin(ang) * sign_ref[...])
    cs_ref[...] = cs
    misc = misc_ref[...]
    t = misc[:, :LANES] * cs
    rot = t + pltpu.roll(t, MLA_ROPE, axis=1)
    krope_ref[...] = jnp.where(lane < MLA_ROPE, rot, 0.0).astype(BF16)
    sc = GATE_SOFTCAP * jnp.tanh((misc[:, LANES:] + bias_ref[...]) / GATE_SOFTCAP)
    log_sig = jnp.minimum(sc, 0.0) - jnp.log(1.0 + jnp.exp(-jnp.abs(sc)))
    gates_ref[...] = jnp.where(lane < ML_HEADS, sc, log_sig)


def _mixer_prep(aux, pos, inv, sign, g_q, g_kv, bias, layer, *, tm=512):
    m = aux.shape[0]
    row = lambda w, c: pl.BlockSpec((tm, w), lambda i: (i, c))
    const = lambda w: pl.BlockSpec((1, w), lambda i: (0, 0))
    par = lambda w: pl.BlockSpec((None, 1, w), lambda i: (layer, 0, 0))
    return pl.pallas_call(
        _prep_kernel,
        grid=(m // tm,),
        in_specs=[row(Q_LORA, 0), row(KV_LORA, Q_LORA // KV_LORA), row(2 * LANES, (Q_LORA + KV_LORA) // (2 * LANES)),
                  pl.BlockSpec((tm, 1), lambda i: (i, 0)), const(LANES), const(LANES),
                  par(Q_LORA), par(KV_LORA), par(LANES)],
        out_specs=[row(Q_LORA, 0), row(KV_LORA, 0), row(LANES, 0), row(LANES, 0), row(LANES, 0)],
        out_shape=[jax.ShapeDtypeStruct((m, Q_LORA), BF16), jax.ShapeDtypeStruct((m, KV_LORA), BF16),
                   jax.ShapeDtypeStruct((m, LANES), BF16), jax.ShapeDtypeStruct((m, LANES), F32),
                   jax.ShapeDtypeStruct((m, LANES), F32)],
        compiler_params=_params(1),
        name="mixer_prep",
    )(aux, aux, aux, pos, inv, sign, g_q.reshape(DEPTH, 1, Q_LORA), g_kv.reshape(DEPTH, 1, KV_LORA), bias)


def _q_epilogue(acc, cs_ref):
    scale = MLA_QK ** -0.5
    cs = cs_ref[...]
    outs = []
    for h in range(acc.shape[1] // MLA_QPAD):
        base = h * MLA_QPAD
        outs.append(acc[:, base:base + MLA_NOPE] * scale)
        u = acc[:, base + MLA_NOPE:base + MLA_QPAD] * cs
        outs.append((u + pltpu.roll(u, MLA_ROPE, axis=1)) * scale)
    return jnp.concatenate(outs, axis=1)


def _kv_kernel(x_ref, w_ref, krope_ref, k_ref, v_ref, wb_ref):
    @pl.when(pl.program_id(1) == 0)
    def _():
        _cast_weight(w_ref, wb_ref)

    r = jnp.dot(x_ref[...], wb_ref[...], preferred_element_type=F32)
    kr = krope_ref[...]
    per_head = MLA_NOPE + MLA_DV
    for h in range(r.shape[1] // per_head):
        k_ref[:, h * MLA_QPAD:h * MLA_QPAD + MLA_NOPE] = r[:, h * per_head:h * per_head + MLA_NOPE].astype(BF16)
        k_ref[:, h * MLA_QPAD + MLA_NOPE:(h + 1) * MLA_QPAD] = kr
        v_ref[:, h * MLA_DV:(h + 1) * MLA_DV] = r[:, h * per_head + MLA_NOPE:(h + 1) * per_head].astype(BF16)


def _kv_up(ckvn, w_ukv, krope, layer, *, tm=1024, heads=2):
    m = ckvn.shape[0]
    tn = heads * (MLA_NOPE + MLA_DV)
    return pl.pallas_call(
        _kv_kernel,
        grid=(MLA_HEADS // heads, m // tm),
        in_specs=[pl.BlockSpec((tm, KV_LORA), lambda j, i: (i, 0)),
                  pl.BlockSpec((None, KV_LORA, tn), lambda j, i: (layer, 0, j)),
                  pl.BlockSpec((tm, LANES), lambda j, i: (i, 0))],
        out_specs=[pl.BlockSpec((tm, heads * MLA_QPAD), lambda j, i: (i, j)),
                   pl.BlockSpec((tm, heads * MLA_DV), lambda j, i: (i, j))],
        out_shape=[jax.ShapeDtypeStruct((m, MLA_HEADS * MLA_QPAD), BF16),
                   jax.ShapeDtypeStruct((m, MLA_WIDTH), BF16)],
        scratch_shapes=[pltpu.VMEM((KV_LORA, tn), BF16)],
        compiler_params=_params(2),
        name="kv_up",
    )(ckvn, w_ukv, krope)


def _mlstm_kernel(q_ref, k_ref, v_ref, o_ref, gc_ref, gr_ref, ng_ref, y_ref, c_ref, n_ref, m_ref):
    L = q_ref.shape[0]
    scale = ML_DQK ** -0.5

    @pl.when(pl.program_id(0) == 0)
    def _():
        c_ref[...] = jnp.zeros_like(c_ref)
        n_ref[...] = jnp.zeros_like(n_ref)
        m_ref[...] = jnp.zeros_like(m_ref)

    gc = gc_ref[...]
    gr = gr_ref[...]
    row = lax.broadcasted_iota(jnp.int32, (L, L), 0)
    col = lax.broadcasted_iota(jnp.int32, (L, L), 1)
    causal = col <= row
    lower = causal.astype(F32)
    upper = (row <= col).astype(F32)
    b_col = jnp.dot(lower, gc, preferred_element_type=F32, precision=lax.Precision.HIGHEST)
    b_row = jnp.dot(gr, upper, preferred_element_type=F32, precision=lax.Precision.HIGHEST)

    for h in range(ML_HEADS):
        bc = b_col[:, ML_HEADS + h:ML_HEADS + h + 1]
        br = b_row[ML_HEADS + h:ML_HEADS + h + 1, :]
        lic = gc[:, h:h + 1]
        lir = gr[h:h + 1, :]
        m_prev = m_ref[h]
        d = jnp.where(causal, bc - br + lir, -jnp.inf)
        a_inter = bc + m_prev
        m_t = jnp.maximum(a_inter, jnp.max(d, axis=-1, keepdims=True))
        w_inter = jnp.exp(a_inter - m_t)
        qh = q_ref[:, h * ML_DQK:(h + 1) * ML_DQK]
        kh = k_ref[:, h * ML_DQK:(h + 1) * ML_DQK]
        vh = v_ref[:, h * ML_DV:(h + 1) * ML_DV]
        s = lax.dot_general(qh, kh, (((1,), (1,)), ((), ())), preferred_element_type=F32)
        p = s * scale * jnp.exp(d - m_t)
        c_prev = c_ref[h]
        n_prev = n_ref[h]
        num = w_inter * jnp.dot(qh, c_prev.astype(BF16), preferred_element_type=F32)
        num = num + jnp.dot(p.astype(BF16), vh, preferred_element_type=F32)
        den = w_inter * jnp.sum(qh.astype(F32) * n_prev, axis=-1, keepdims=True) + jnp.sum(p, axis=-1, keepdims=True)
        hout = num / jnp.maximum(jnp.abs(den), jnp.exp(-m_t))
        m_new = m_t[L - 1:L, :]
        b_last = bc[L - 1:L, :]
        w_prev = jnp.exp(b_last + m_prev - m_new)
        ws_col = jnp.exp(b_last - bc + lic - m_new) * scale
        kw = kh.astype(F32) * ws_col
        c_ref[h] = w_prev * c_prev + lax.dot_general(kw.astype(BF16), vh, (((0,), (0,)), ((), ())),
                                                     preferred_element_type=F32)
        n_ref[h] = w_prev * n_prev + jnp.sum(kw, axis=0, keepdims=True)
        m_ref[h] = m_new
        hn = hout * lax.rsqrt(jnp.mean(hout * hout, axis=-1, keepdims=True) + RMS_EPS)
        hn = hn * ng_ref[:, h * ML_DV:(h + 1) * ML_DV]
        gate = jax.nn.sigmoid(o_ref[:, h * ML_DV:(h + 1) * ML_DV].astype(F32))
        y_ref[:, h * ML_DV:(h + 1) * ML_DV] = (hn * gate).astype(BF16)


def _mlstm(qkvo, gates_col, gates_row, norm_g, layer, *, chunk=ML_CHUNK):
    s = qkvo.shape[0]
    hq = ML_HEADS * ML_DQK
    return pl.pallas_call(
        _mlstm_kernel,
        grid=(s // chunk,),
        in_specs=[pl.BlockSpec((chunk, hq), lambda c: (c, 0)),
                  pl.BlockSpec((chunk, hq), lambda c: (c, 1)),
                  pl.BlockSpec((chunk, ML_WIDTH), lambda c: (c, 2 * hq // ML_WIDTH)),
                  pl.BlockSpec((chunk, ML_WIDTH), lambda c: (c, 2 * hq // ML_WIDTH + 1)),
                  pl.BlockSpec((chunk, LANES), lambda c: (c, 0)),
                  pl.BlockSpec((LANES, chunk), lambda c: (0, c)),
                  pl.BlockSpec((None, 1, ML_WIDTH), lambda c: (layer, 0, 0))],
        out_specs=pl.BlockSpec((chunk, ML_WIDTH), lambda c: (c, 0)),
        out_shape=jax.ShapeDtypeStruct((s, ML_WIDTH), BF16),
        scratch_shapes=[pltpu.VMEM((ML_HEADS, ML_DQK, ML_DV), F32), pltpu.VMEM((ML_HEADS, 1, ML_DQK), F32),
                        pltpu.VMEM((ML_HEADS, 1, 1), F32)],
        compiler_params=_params(1),
        name="mlstm",
    )(qkvo, qkvo, qkvo, qkvo, gates_col, gates_row, norm_g.reshape(DEPTH, 1, ML_WIDTH))


def _mla_kernel(qi_ref, ki_ref, q_ref, k_ref, v_ref, o_ref, m_ref, l_ref, acc_ref, *, tq, tk):
    p_id = pl.program_id(1)
    qi = qi_ref[p_id]
    ki = ki_ref[p_id]

    @pl.when(ki == 0)
    def _():
        m_ref[...] = jnp.full_like(m_ref, -jnp.inf)
        l_ref[...] = jnp.zeros_like(l_ref)
        acc_ref[...] = jnp.zeros_like(acc_ref)

    s = lax.dot_general(q_ref[...], k_ref[...], (((1,), (1,)), ((), ())), preferred_element_type=F32)
    last = (ki + 1) * tk >= (qi + 1) * tq

    def update(s):
        m_prev = m_ref[...]
        m_new = jnp.maximum(m_prev, jnp.max(s, axis=-1, keepdims=True))
        alpha = jnp.exp(m_prev - m_new)
        p = jnp.exp(s - m_new)
        l_ref[...] = alpha * l_ref[...] + jnp.sum(p, axis=-1, keepdims=True)
        acc_ref[...] = alpha * acc_ref[...] + jnp.dot(p.astype(BF16), v_ref[...], preferred_element_type=F32)
        m_ref[...] = m_new

    @pl.when(jnp.logical_not(last))
    def _():
        update(s)

    @pl.when(last)
    def _():
        q_chunk = (qi * tq + lax.broadcasted_iota(jnp.int32, (tq, tk), 0)) // CHUNK
        k_chunk = (ki * tk + lax.broadcasted_iota(jnp.int32, (tq, tk), 1)) // CHUNK
        update(jnp.where(k_chunk <= q_chunk, s, -jnp.inf))
        o_ref[...] = (acc_ref[...] / l_ref[...]).astype(BF16)


def _mla_attention(q, k, v, *, tq=1024, tk=1024):
    s = q.shape[0]
    assert tq % tk == 0 and tk % CHUNK == 0
    pairs = [(a, b) for a in range(s // tq) for b in range((a + 1) * tq // tk)]
    qi_tab = jnp.asarray([a for a, _ in pairs], jnp.int32)
    ki_tab = jnp.asarray([b for _, b in pairs], jnp.int32)
    grid_spec = pltpu.PrefetchScalarGridSpec(
        num_scalar_prefetch=2,
        grid=(MLA_HEADS, len(pairs)),
        in_specs=[pl.BlockSpec((tq, MLA_QPAD), lambda h, p, qt, kt: (qt[p], h)),
                  pl.BlockSpec((tk, MLA_QPAD), lambda h, p, qt, kt: (kt[p], h)),
                  pl.BlockSpec((tk, MLA_DV), lambda h, p, qt, kt: (kt[p], h))],
        out_specs=pl.BlockSpec((tq, MLA_DV), lambda h, p, qt, kt: (qt[p], h)),
        scratch_shapes=[pltpu.VMEM((tq, 1), F32), pltpu.VMEM((tq, 1), F32), pltpu.VMEM((tq, MLA_DV), F32)],
    )
    return pl.pallas_call(
        functools.partial(_mla_kernel, tq=tq, tk=tk),
        grid_spec=grid_spec,
        out_shape=jax.ShapeDtypeStruct((s, MLA_WIDTH), BF16),
        compiler_params=_params(2),
        name="mla_attn",
    )(qi_tab, ki_tab, q, k, v)


def _xattn_kernel(q_ref, k_ref, v_ref, o_ref):
    for h in range(X_HEADS):
        sl = slice(h * X_HEAD_DIM, (h + 1) * X_HEAD_DIM)
        s = lax.dot_general(q_ref[:, sl], k_ref[:, sl], (((1,), (1,)), ((), ())), preferred_element_type=F32)
        p = jnp.exp(s - jnp.max(s, axis=-1, keepdims=True))
        o = jnp.dot(p.astype(BF16), v_ref[:, sl], preferred_element_type=F32)
        o_ref[:, sl] = (o / jnp.sum(p, axis=-1, keepdims=True)).astype(BF16)


def _xattn(q, kv, *, tm=512):
    s = q.shape[0]
    n_mem = kv.shape[0]
    return pl.pallas_call(
        _xattn_kernel,
        grid=(s // tm,),
        in_specs=[pl.BlockSpec((tm, X_WIDTH), lambda i: (i, 0)),
                  pl.BlockSpec((n_mem, X_WIDTH), lambda i: (0, 0)),
                  pl.BlockSpec((n_mem, X_WIDTH), lambda i: (0, 1))],
        out_specs=pl.BlockSpec((tm, X_WIDTH), lambda i: (i, 0)),
        out_shape=jax.ShapeDtypeStruct((s, X_WIDTH), BF16),
        compiler_params=_params(1),
        name="xattn",
    )(q, kv, kv)


def _router_kernel(x_ref, w_ref, b_ref, o_ref):
    o_ref[...] = jnp.dot(x_ref[...], w_ref[...], preferred_element_type=F32,
                         precision=lax.Precision.HIGHEST) + b_ref[...]


def _router(x, w_router, b_router, layer, *, tm=512):
    m, d = x.shape
    return pl.pallas_call(
        _router_kernel,
        grid=(m // tm,),
        in_specs=[pl.BlockSpec((tm, d), lambda i: (i, 0)),
                  pl.BlockSpec((None, d, N_EXPERTS), lambda i: (layer, 0, 0)),
                  pl.BlockSpec((None, 1, N_EXPERTS), lambda i: (layer, 0, 0))],
        out_specs=pl.BlockSpec((tm, N_EXPERTS), lambda i: (i, 0)),
        out_shape=jax.ShapeDtypeStruct((m, N_EXPERTS), F32),
        compiler_params=_params(1),
        name="router",
    )(x, w_router, b_router.reshape(DEPTH, 1, N_EXPERTS))


def _expert_changed(be_ref, b):
    prev = be_ref[jnp.maximum(b - 1, 0)]
    return jnp.logical_or(b == 0, be_ref[b] != prev)


def _gu_kernel(be_ref, nu_ref, x_ref, wg_ref, wu_ref, bg_ref, bu_ref, h_ref, wgb_ref, wub_ref):
    b = pl.program_id(1)

    @pl.when(_expert_changed(be_ref, b))
    def _():
        _cast_weight(wg_ref, wgb_ref)
        _cast_weight(wu_ref, wub_ref)

    @pl.when(b < nu_ref[0])
    def _():
        x = x_ref[...]
        g = jnp.dot(x, wgb_ref[...], preferred_element_type=F32) + bg_ref[...]
        u = jnp.dot(x, wub_ref[...], preferred_element_type=F32) + bu_ref[...]
        g = jnp.minimum(g, SWIGLU_LIMIT)
        u = jnp.clip(u, -SWIGLU_LIMIT, SWIGLU_LIMIT)
        h_ref[...] = ((u + 1.0) * g * jax.nn.sigmoid(SWIGLU_ALPHA * g)).astype(BF16)


def _expert_gu(xg, w_gu, b_gu, block_expert, n_used, layer, *, tf=256):
    p, d = xg.shape
    nb = p // MOE_BLOCK
    nt = D_EXPERT // tf
    live = lambda b, nu: jnp.minimum(b, nu[0] - 1)
    grid_spec = pltpu.PrefetchScalarGridSpec(
        num_scalar_prefetch=2,
        grid=(nt, nb),
        in_specs=[pl.BlockSpec((MOE_BLOCK, d), lambda t, b, be, nu: (live(b, nu), 0)),
                  pl.BlockSpec((None, None, d, tf), lambda t, b, be, nu: (layer, be[b], 0, t)),
                  pl.BlockSpec((None, None, d, tf), lambda t, b, be, nu: (layer, be[b], 0, nt + t)),
                  pl.BlockSpec((None, None, 1, tf), lambda t, b, be, nu: (layer, be[b], 0, t)),
                  pl.BlockSpec((None, None, 1, tf), lambda t, b, be, nu: (layer, be[b], 0, nt + t))],
        out_specs=pl.BlockSpec((MOE_BLOCK, tf), lambda t, b, be, nu: (b, t)),
        scratch_shapes=[pltpu.VMEM((d, tf), BF16), pltpu.VMEM((d, tf), BF16)],
    )
    b_gu4 = b_gu.reshape(DEPTH, N_EXPERTS, 1, 2 * D_EXPERT)
    return pl.pallas_call(
        _gu_kernel,
        grid_spec=grid_spec,
        out_shape=jax.ShapeDtypeStruct((p, D_EXPERT), BF16),
        compiler_params=_params(2),
        name="expert_gu",
    )(block_expert, n_used, xg, w_gu, w_gu, b_gu4, b_gu4)


def _down_kernel(be_ref, nu_ref, h_ref, w_ref, bd_ref, y_ref, wb_ref):
    b = pl.program_id(1)

    @pl.when(_expert_changed(be_ref, b))
    def _():
        _cast_weight(w_ref, wb_ref, rows=256)

    @pl.when(b < nu_ref[0])
    def _():
        y_ref[...] = jnp.dot(h_ref[...], wb_ref[...], preferred_element_type=F32) + bd_ref[...]


def _expert_down(h, w_down, b_down, block_expert, n_used, layer, *, tn=1024):
    p, f = h.shape
    nb = p // MOE_BLOCK
    d = w_down.shape[-1]
    live = lambda b, nu: jnp.minimum(b, nu[0] - 1)
    grid_spec = pltpu.PrefetchScalarGridSpec(
        num_scalar_prefetch=2,
        grid=(d // tn, nb),
        in_specs=[pl.BlockSpec((MOE_BLOCK, f), lambda t, b, be, nu: (live(b, nu), 0)),
                  pl.BlockSpec((None, None, f, tn), lambda t, b, be, nu: (layer, be[b], 0, t)),
                  pl.BlockSpec((None, None, 1, tn), lambda t, b, be, nu: (layer, be[b], 0, t))],
        out_specs=pl.BlockSpec((MOE_BLOCK, tn), lambda t, b, be, nu: (b, t)),
        scratch_shapes=[pltpu.VMEM((f, tn), BF16)],
    )
    return pl.pallas_call(
        _down_kernel,
        grid_spec=grid_spec,
        out_shape=jax.ShapeDtypeStruct((p, d), F32),
        compiler_params=_params(2),
        name="expert_down",
    )(block_expert, n_used, h, w_down, b_down.reshape(DEPTH, N_EXPERTS, 1, d))


def _route(logits):
    t = logits.shape[0]
    top_val, top_idx = lax.top_k(logits, TOP_K)
    gates = jax.nn.softmax(top_val, axis=-1)
    a = t * TOP_K
    e_flat = top_idx.reshape(a)
    order = jnp.argsort(e_flat)
    e_sorted = e_flat[order]
    counts = jnp.bincount(e_flat, length=N_EXPERTS)
    starts = jnp.cumsum(counts) - counts
    padded = (counts + MOE_BLOCK - 1) // MOE_BLOCK * MOE_BLOCK
    pad_ends = jnp.cumsum(padded)
    pad_starts = pad_ends - padded
    dest_sorted = (pad_starts[e_sorted] + jnp.arange(a) - starts[e_sorted]).astype(jnp.int32)
    nb = a // MOE_BLOCK + N_EXPERTS
    dest = jnp.zeros((a,), jnp.int32).at[order].set(dest_sorted)
    row_tok = jnp.full((nb * MOE_BLOCK,), t, jnp.int32).at[dest_sorted].set((order // TOP_K).astype(jnp.int32))
    block_expert = jnp.clip(jnp.searchsorted(pad_ends, jnp.arange(nb) * MOE_BLOCK, side='right'),
                            0, N_EXPERTS - 1).astype(jnp.int32)
    n_used = (pad_ends[-1:] // MOE_BLOCK).astype(jnp.int32)
    return gates, dest.reshape(t, TOP_K), row_tok, block_expert, n_used


def _swap_halves(w):
    half = w.shape[-1] // 2
    return jnp.concatenate([w[..., half:], w[..., :half]], axis=-1)


def kernel(x, mem, positions, w_in, ml_b_i, ml_b_f, ml_norm_g, mla_g_q, mla_w_uq, mla_g_kv, mla_w_ukv, w_out, ln1_g, ln1_b, x_g_mem, x_b_mem, x_w_q, x_w_kv, x_w_o, ln2_g, ln2_b, w_router, b_router, w_gu, b_gu, w_down, b_down, ln3_g, ln3_b):
    batch, seq, d = x.shape
    assert batch == 1
    t = batch * seq
    x = x.reshape(t, d)
    mem2 = mem.reshape(-1, d)
    pos = positions.reshape(t, 1)

    kr_w = w_in[:, :, KR_OFF:KR_OFF + MLA_ROPE]
    w_aux = jnp.concatenate(
        [w_in[:, :, CQ_OFF:KR_OFF], kr_w, _swap_halves(kr_w), w_in[:, :, GATES_OFF:CQ_OFF],
         jnp.zeros((DEPTH, d, AUX_WIDTH - (Q_LORA + KV_LORA + 2 * MLA_ROPE + 2 * ML_HEADS)), w_in.dtype)], axis=-1)
    uq = mla_w_uq.reshape(DEPTH, Q_LORA, MLA_HEADS, MLA_QK)
    uq_rope = uq[..., MLA_NOPE:]
    w_uq = jnp.concatenate([uq[..., :MLA_NOPE], uq_rope, _swap_halves(uq_rope)], axis=-1)
    w_uq = w_uq.reshape(DEPTH, Q_LORA, MLA_HEADS * MLA_QPAD)
    gate_bias = jnp.concatenate([ml_b_i, ml_b_f, jnp.zeros((DEPTH, LANES - 2 * ML_HEADS), F32)], axis=-1)
    gate_bias = gate_bias.reshape(DEPTH, 1, LANES)
    half = MLA_ROPE // 2
    inv = ROPE_THETA ** (-jnp.arange(half, dtype=F32) / half)
    inv = jnp.tile(inv, 4).reshape(1, LANES)
    sign = jnp.concatenate([jnp.ones((2 * half,), F32), -jnp.ones((half,), F32), jnp.ones((half,), F32)])
    sign = sign.reshape(1, LANES)

    xb = x.astype(BF16)
    for l in range(DEPTH):
        qkvo = _matmul([xb], w_in, l, tm=1024, tn=512, n_out=QKVO_WIDTH, name="in_proj")
        aux = _matmul([xb], w_aux, l, tm=1024, tn=256, n_out=AUX_WIDTH, out_dtype=F32, name="aux_proj")
        cqn, ckvn, krope, cs, gates = _mixer_prep(aux, pos, inv, sign, mla_g_q, mla_g_kv, gate_bias, l)
        y_ml = _mlstm(qkvo, gates, gates.T, ml_norm_g, l)
        q = _matmul([cqn], w_uq, l, tm=1024, tn=512, n_out=MLA_HEADS * MLA_QPAD, epilogue=_q_epilogue,
                    extra=(cs,), extra_specs=(pl.BlockSpec((1024, LANES), lambda j, i: (i, 0)),), name="q_up")
        k, v = _kv_up(ckvn, mla_w_ukv, krope, l)
        y_mla = _mla_attention(q, k, v)
        h = _matmul([y_ml, y_mla], w_out, l, tm=1024, tn=512, n_out=d, name="out_proj")
        x, xb = _res_ln(x, h, ln1_g, ln1_b, l)
        mem_n = _ln_bf16(mem2, x_g_mem, x_b_mem, l)
        kv = _matmul([mem_n], x_w_kv, l, tm=mem_n.shape[0], tn=512, n_out=2 * X_WIDTH, name="mem_kv")
        qx = _matmul([xb], x_w_q, l, tm=1024, tn=512, n_out=X_WIDTH,
                     epilogue=lambda acc: acc * (X_HEAD_DIM ** -0.5), name="xq_proj")
        ox = _xattn(qx, kv)
        h = _matmul([ox], x_w_o, l, tm=1024, tn=512, n_out=d, name="xo_proj")
        x, xb = _res_ln(x, h, ln2_g, ln2_b, l)
        logits = _router(x, w_router, b_router, l)
        gates_k, dest, row_tok, block_expert, n_used = _route(logits)
        xg = jnp.concatenate([xb, jnp.zeros((1, d), BF16)], axis=0)[row_tok]
        hmid = _expert_gu(xg, w_gu, b_gu, block_expert, n_used, l)
        y_rows = _expert_down(hmid, w_down, b_down, block_expert, n_used, l)
        h = jnp.sum(y_rows[dest] * gates_k[:, :, None], axis=1).astype(BF16)
        x, xb = _res_ln(x, h, ln3_g, ln3_b, l)
    return x.reshape(batch, seq, d)
```

```python
import functools

import jax
import jax.numpy as jnp
from jax import lax
from jax.experimental import pallas as pl
from jax.experimental.pallas import tpu as pltpu

D_MODEL = 4096
DEPTH = 2
CHUNK = 64
ML_HEADS = 8
ML_DQK = 128
ML_DV = 256
ML_WIDTH = ML_HEADS * ML_DV
GATE_SOFTCAP = 15.0
MLA_HEADS = 16
MLA_NOPE = 128
MLA_ROPE = 64
MLA_DV = 128
MLA_QK = MLA_NOPE + MLA_ROPE
MLA_WIDTH = MLA_HEADS * MLA_DV
MLA_QPAD = 256
MLA_VPAD = 256
LOG2_E = 1.4426950408889634
Q_LORA = 1024
KV_LORA = 512
ROPE_THETA = 10000.0
X_HEADS = 4
X_HEAD_DIM = 256
X_WIDTH = X_HEADS * X_HEAD_DIM
N_EXPERTS = 32
TOP_K = 4
D_EXPERT = 768
SWIGLU_LIMIT = 7.0
SWIGLU_ALPHA = 1.702
DN_ALPHA = (2.0 * DEPTH) ** 0.25
LN_EPS = 1e-5
RMS_EPS = 1e-6
QKVO_WIDTH = 2 * ML_HEADS * ML_DQK + 2 * ML_WIDTH
GATES_OFF = QKVO_WIDTH
CQ_OFF = GATES_OFF + 2 * ML_HEADS
CKV_OFF = CQ_OFF + Q_LORA
KR_OFF = CKV_OFF + KV_LORA
AUX_WIDTH = 1792

VMEM_LIMIT = 56 * 1024 * 1024
MOE_BLOCK = 256
LANES = 128
SLAB = D_MODEL // 2 // LANES
ML_CHUNK = 128

BF16 = jnp.bfloat16
F32 = jnp.float32


def _params(n_axes, vmem=VMEM_LIMIT):
    return pltpu.CompilerParams(dimension_semantics=("arbitrary",) * n_axes, vmem_limit_bytes=vmem)


def _cast_weight(w_ref, wb_ref, rows=512):
    k = w_ref.shape[0]
    rows = min(rows, k)

    def body(c, carry):
        sl = pl.ds(pl.multiple_of(c * rows, rows), rows)
        wb_ref[sl, :] = w_ref[sl, :].astype(BF16)
        return carry

    lax.fori_loop(0, k // rows, body, 0)


def _mm_kernel(*refs, n_x, epilogue):
    x_refs = refs[:n_x]
    w_ref = refs[n_x]
    extra = refs[n_x + 1:-2]
    o_ref, wb_ref = refs[-2:]

    @pl.when(pl.program_id(1) == 0)
    def _():
        _cast_weight(w_ref, wb_ref)

    off = 0
    acc = None
    for x_ref in x_refs:
        kx = x_ref.shape[1]
        part = jnp.dot(x_ref[...], wb_ref[off:off + kx, :], preferred_element_type=F32)
        acc = part if acc is None else acc + part
        off += kx
    if epilogue is not None:
        acc = epilogue(acc, *extra)
    o_ref[...] = acc.astype(o_ref.dtype)


def _matmul(xs, w, layer, *, tm, tn, n_out, col_off=0, out_dtype=BF16, epilogue=None, extra=(), extra_specs=(),
            name="mm"):
    m = xs[0].shape[0]
    k = sum(x.shape[1] for x in xs)
    assert w.shape[1] == k and m % tm == 0 and n_out % tn == 0 and col_off % tn == 0
    cb = col_off // tn
    in_specs = [pl.BlockSpec((tm, x.shape[1]), lambda j, i: (i, 0)) for x in xs]
    in_specs.append(pl.BlockSpec((None, k, tn), lambda j, i: (layer, 0, j + cb)))
    in_specs.extend(extra_specs)
    return pl.pallas_call(
        functools.partial(_mm_kernel, n_x=len(xs), epilogue=epilogue),
        grid=(n_out // tn, m // tm),
        in_specs=in_specs,
        out_specs=pl.BlockSpec((tm, tn), lambda j, i: (i, j)),
        out_shape=jax.ShapeDtypeStruct((m, n_out), out_dtype),
        scratch_shapes=[pltpu.VMEM((k, tn), BF16)],
        compiler_params=_params(2),
        name=name,
    )(*xs, w, *extra)


def _ln_rows(v, g, b):
    mu = jnp.mean(v, axis=-1, keepdims=True)
    c = v - mu
    var = jnp.mean(c * c, axis=-1, keepdims=True)
    return c * lax.rsqrt(var + LN_EPS) * g + b


def _res_ln_kernel(x_ref, h_ref, g_ref, b_ref, o_ref, ob_ref):
    v = DN_ALPHA * x_ref[...] + h_ref[...].astype(F32)
    y = _ln_rows(v, g_ref[...], b_ref[...])
    o_ref[...] = y
    ob_ref[...] = y.astype(BF16)


def _res_ln(x, h, g, b, layer, *, tm=256):
    m, d = x.shape
    row = pl.BlockSpec((tm, d), lambda i: (i, 0))
    par = pl.BlockSpec((None, 1, d), lambda i: (layer, 0, 0))
    return pl.pallas_call(
        _res_ln_kernel,
        grid=(m // tm,),
        in_specs=[row, row, par, par],
        out_specs=[row, row],
        out_shape=[jax.ShapeDtypeStruct((m, d), F32), jax.ShapeDtypeStruct((m, d), BF16)],
        compiler_params=_params(1),
        name="res_ln",
    )(x, h, g.reshape(DEPTH, 1, d), b.reshape(DEPTH, 1, d))


def _pack_slabs(y, slab_ref):
    rows = y.shape[0]
    for j in range(SLAB):
        lo = y[:, 2 * j * LANES:(2 * j + 1) * LANES]
        hi = y[:, (2 * j + 1) * LANES:(2 * j + 2) * LANES]
        slab_ref[pl.ds(j, rows, stride=SLAB), :] = pltpu.pack_elementwise([lo, hi], packed_dtype=BF16)


def _unpack_slab_row(slab_ref, first, rows, j):
    u = slab_ref[pl.ds(first + j, rows, stride=SLAB), :]
    lo = pltpu.unpack_elementwise(u, index=0, packed_dtype=BF16, unpacked_dtype=F32)
    hi = pltpu.unpack_elementwise(u, index=1, packed_dtype=BF16, unpacked_dtype=F32)
    return lo, hi


def _res_ln_slab_kernel(x_ref, h_ref, g_ref, b_ref, o_ref, xp_ref):
    v = DN_ALPHA * x_ref[...] + h_ref[...].astype(F32)
    y = _ln_rows(v, g_ref[...], b_ref[...])
    o_ref[...] = y
    _pack_slabs(y, xp_ref)


def _res_ln_slab(x, h, g, b, layer, *, tm=256):
    m, d = x.shape
    row = pl.BlockSpec((tm, d), lambda i: (i, 0))
    par = pl.BlockSpec((None, 1, d), lambda i: (layer, 0, 0))
    return pl.pallas_call(
        _res_ln_slab_kernel,
        grid=(m // tm,),
        in_specs=[row, row, par, par],
        out_specs=[row, pl.BlockSpec((tm * SLAB, LANES), lambda i: (i, 0))],
        out_shape=[jax.ShapeDtypeStruct((m, d), F32), jax.ShapeDtypeStruct((m * SLAB, LANES), jnp.uint32)],
        compiler_params=_params(1),
        name="res_ln_slab",
    )(x, h, g.reshape(DEPTH, 1, d), b.reshape(DEPTH, 1, d))


def _ln_kernel(x_ref, g_ref, b_ref, ob_ref):
    ob_ref[...] = _ln_rows(x_ref[...], g_ref[...], b_ref[...]).astype(BF16)


def _ln_bf16(x, g, b, layer, *, tm=256):
    m, d = x.shape
    row = pl.BlockSpec((tm, d), lambda i: (i, 0))
    par = pl.BlockSpec((None, 1, d), lambda i: (layer, 0, 0))
    return pl.pallas_call(
        _ln_kernel,
        grid=(m // tm,),
        in_specs=[row, par, par],
        out_specs=row,
        out_shape=jax.ShapeDtypeStruct((m, d), BF16),
        compiler_params=_params(1),
        name="mem_ln",
    )(x, g.reshape(DEPTH, 1, d), b.reshape(DEPTH, 1, d))


def _prep_kernel(cq_ref, ckv_ref, misc_ref, pos_ref, inv_ref, sign_ref, gq_ref, gkv_ref, bias_ref,
                 cqn_ref, ckvn_ref, krope_ref, cs_ref, gates_ref):
    def rms(v, g):
        return v * lax.rsqrt(jnp.mean(v * v, axis=-1, keepdims=True) + RMS_EPS) * g

    cqn_ref[...] = rms(cq_ref[...], gq_ref[...]).astype(BF16)
    ckvn_ref[...] = rms(ckv_ref[...], gkv_ref[...]).astype(BF16)
    ang = pos_ref[...].astype(F32) * inv_ref[...]
    lane = lax.broadcasted_iota(jnp.int32, ang.shape, 1)
    cs = jnp.where(lane < MLA_ROPE, jnp.cos(ang), jnp.sin(ang) * sign_ref[...])
    cs_ref[...] = cs
    misc = misc_ref[...]
    t = misc[:, :LANES] * cs
    rot = t + pltpu.roll(t, MLA_ROPE, axis=1)
    krope_ref[...] = jnp.where(lane < MLA_ROPE, rot, 0.0).astype(BF16)
    sc = GATE_SOFTCAP * jnp.tanh((misc[:, LANES:] + bias_ref[...]) / GATE_SOFTCAP)
    log_sig = jnp.minimum(sc, 0.0) - jnp.log(1.0 + jnp.exp(-jnp.abs(sc)))
    gates_ref[...] = jnp.where(lane < ML_HEADS, sc, log_sig)


def _mixer_prep(aux, pos, inv, sign, g_q, g_kv, bias, layer, *, tm=512):
    m = aux.shape[0]
    row = lambda w, c: pl.BlockSpec((tm, w), lambda i: (i, c))
    const = lambda w: pl.BlockSpec((1, w), lambda i: (0, 0))
    par = lambda w: pl.BlockSpec((None, 1, w), lambda i: (layer, 0, 0))
    return pl.pallas_call(
        _prep_kernel,
        grid=(m // tm,),
        in_specs=[row(Q_LORA, 0), row(KV_LORA, Q_LORA // KV_LORA), row(2 * LANES, (Q_LORA + KV_LORA) // (2 * LANES)),
                  pl.BlockSpec((tm, 1), lambda i: (i, 0)), const(LANES), const(LANES),
                  par(Q_LORA), par(KV_LORA), par(LANES)],
        out_specs=[row(Q_LORA, 0), row(KV_LORA, 0), row(LANES, 0), row(LANES, 0), row(LANES, 0)],
        out_shape=[jax.ShapeDtypeStruct((m, Q_LORA), BF16), jax.ShapeDtypeStruct((m, KV_LORA), BF16),
                   jax.ShapeDtypeStruct((m, LANES), BF16), jax.ShapeDtypeStruct((m, LANES), F32),
                   jax.ShapeDtypeStruct((m, LANES), F32)],
        compiler_params=_params(1),
        name="mixer_prep",
    )(aux, aux, aux, pos, inv, sign, g_q.reshape(DEPTH, 1, Q_LORA), g_kv.reshape(DEPTH, 1, KV_LORA), bias)


def _q_epilogue(acc, cs_ref):
    scale = MLA_QK ** -0.5 * LOG2_E
    cs = cs_ref[...]
    outs = []
    for h in range(acc.shape[1] // MLA_QPAD):
        base = h * MLA_QPAD
        outs.append(acc[:, base:base + MLA_NOPE] * scale)
        u = acc[:, base + MLA_NOPE:base + MLA_QPAD] * cs
        outs.append((u + pltpu.roll(u, MLA_ROPE, axis=1)) * scale)
    return jnp.concatenate(outs, axis=1)


def _kv_kernel(x_ref, w_ref, krope_ref, k_ref, v_ref, wb_ref):
    @pl.when(pl.program_id(1) == 0)
    def _():
        _cast_weight(w_ref, wb_ref)

    r = jnp.dot(x_ref[...], wb_ref[...], preferred_element_type=F32)
    kr = krope_ref[...]
    per_head = MLA_NOPE + MLA_DV
    for h in range(r.shape[1] // per_head):
        k_ref[:, h * MLA_QPAD:h * MLA_QPAD + MLA_NOPE] = r[:, h * per_head:h * per_head + MLA_NOPE].astype(BF16)
        k_ref[:, h * MLA_QPAD + MLA_NOPE:(h + 1) * MLA_QPAD] = kr
        v_ref[:, h * MLA_VPAD:h * MLA_VPAD + MLA_DV] = r[:, h * per_head + MLA_NOPE:(h + 1) * per_head].astype(BF16)
        v_ref[:, h * MLA_VPAD + MLA_DV:(h + 1) * MLA_VPAD] = jnp.ones((r.shape[0], MLA_VPAD - MLA_DV), BF16)


def _kv_up(ckvn, w_ukv, krope, layer, *, tm=1024, heads=2):
    m = ckvn.shape[0]
    tn = heads * (MLA_NOPE + MLA_DV)
    return pl.pallas_call(
        _kv_kernel,
        grid=(MLA_HEADS // heads, m // tm),
        in_specs=[pl.BlockSpec((tm, KV_LORA), lambda j, i: (i, 0)),
                  pl.BlockSpec((None, KV_LORA, tn), lambda j, i: (layer, 0, j)),
                  pl.BlockSpec((tm, LANES), lambda j, i: (i, 0))],
        out_specs=[pl.BlockSpec((tm, heads * MLA_QPAD), lambda j, i: (i, j)),
                   pl.BlockSpec((tm, heads * MLA_VPAD), lambda j, i: (i, j))],
        out_shape=[jax.ShapeDtypeStruct((m, MLA_HEADS * MLA_QPAD), BF16),
                   jax.ShapeDtypeStruct((m, MLA_HEADS * MLA_VPAD), BF16)],
        scratch_shapes=[pltpu.VMEM((KV_LORA, tn), BF16)],
        compiler_params=_params(2),
        name="kv_up",
    )(ckvn, w_ukv, krope)


def _mlstm_kernel(q_ref, k_ref, v_ref, o_ref, gc_ref, gr_ref, ng_ref, y_ref, c_ref, n_ref, m_ref):
    L = q_ref.shape[0]
    scale = ML_DQK ** -0.5

    @pl.when(pl.program_id(0) == 0)
    def _():
        c_ref[...] = jnp.zeros_like(c_ref)
        n_ref[...] = jnp.zeros_like(n_ref)
        m_ref[...] = jnp.zeros_like(m_ref)

    gc = gc_ref[...]
    gr = gr_ref[...]
    row = lax.broadcasted_iota(jnp.int32, (L, L), 0)
    col = lax.broadcasted_iota(jnp.int32, (L, L), 1)
    causal = col <= row
    lower = causal.astype(F32)
    upper = (row <= col).astype(F32)
    b_col = jnp.dot(lower, gc, preferred_element_type=F32, precision=lax.Precision.HIGHEST)
    b_row = jnp.dot(gr, upper, preferred_element_type=F32, precision=lax.Precision.HIGHEST)

    for h in range(ML_HEADS):
        bc = b_col[:, ML_HEADS + h:ML_HEADS + h + 1]
        br = b_row[ML_HEADS + h:ML_HEADS + h + 1, :]
        lic = gc[:, h:h + 1]
        lir = gr[h:h + 1, :]
        m_prev = m_ref[h]
        d = jnp.where(causal, bc - br + lir, -jnp.inf)
        a_inter = bc + m_prev
        m_t = jnp.maximum(a_inter, jnp.max(d, axis=-1, keepdims=True))
        w_inter = jnp.exp(a_inter - m_t)
        qh = q_ref[:, h * ML_DQK:(h + 1) * ML_DQK]
        kh = k_ref[:, h * ML_DQK:(h + 1) * ML_DQK]
        vh = v_ref[:, h * ML_DV:(h + 1) * ML_DV]
        s = lax.dot_general(qh, kh, (((1,), (1,)), ((), ())), preferred_element_type=F32)
        p = s * scale * jnp.exp(d - m_t)
        c_prev = c_ref[h]
        n_prev = n_ref[h]
        num = w_inter * jnp.dot(qh, c_prev.astype(BF16), preferred_element_type=F32)
        num = num + jnp.dot(p.astype(BF16), vh, preferred_element_type=F32)
        den = w_inter * jnp.sum(qh.astype(F32) * n_prev, axis=-1, keepdims=True) + jnp.sum(p, axis=-1, keepdims=True)
        hout = num / jnp.maximum(jnp.abs(den), jnp.exp(-m_t))
        m_new = m_t[L - 1:L, :]
        b_last = bc[L - 1:L, :]
        w_prev = jnp.exp(b_last + m_prev - m_new)
        ws_col = jnp.exp(b_last - bc + lic - m_new) * scale
        kw = kh.astype(F32) * ws_col
        c_ref[h] = w_prev * c_prev + lax.dot_general(kw.astype(BF16), vh, (((0,), (0,)), ((), ())),
                                                     preferred_element_type=F32)
        n_ref[h] = w_prev * n_prev + jnp.sum(kw, axis=0, keepdims=True)
        m_ref[h] = m_new
        hn = hout * lax.rsqrt(jnp.mean(hout * hout, axis=-1, keepdims=True) + RMS_EPS)
        hn = hn * ng_ref[:, h * ML_DV:(h + 1) * ML_DV]
        gate = jax.nn.sigmoid(o_ref[:, h * ML_DV:(h + 1) * ML_DV].astype(F32))
        y_ref[:, h * ML_DV:(h + 1) * ML_DV] = (hn * gate).astype(BF16)


def _mlstm(qkvo, gates_col, gates_row, norm_g, layer, *, chunk=ML_CHUNK):
    s = qkvo.shape[0]
    hq = ML_HEADS * ML_DQK
    return pl.pallas_call(
        _mlstm_kernel,
        grid=(s // chunk,),
        in_specs=[pl.BlockSpec((chunk, hq), lambda c: (c, 0)),
                  pl.BlockSpec((chunk, hq), lambda c: (c, 1)),
                  pl.BlockSpec((chunk, ML_WIDTH), lambda c: (c, 2 * hq // ML_WIDTH)),
                  pl.BlockSpec((chunk, ML_WIDTH), lambda c: (c, 2 * hq // ML_WIDTH + 1)),
                  pl.BlockSpec((chunk, LANES), lambda c: (c, 0)),
                  pl.BlockSpec((LANES, chunk), lambda c: (0, c)),
                  pl.BlockSpec((None, 1, ML_WIDTH), lambda c: (layer, 0, 0))],
        out_specs=pl.BlockSpec((chunk, ML_WIDTH), lambda c: (c, 0)),
        out_shape=jax.ShapeDtypeStruct((s, ML_WIDTH), BF16),
        scratch_shapes=[pltpu.VMEM((ML_HEADS, ML_DQK, ML_DV), F32), pltpu.VMEM((ML_HEADS, 1, ML_DQK), F32),
                        pltpu.VMEM((ML_HEADS, 1, 1), F32)],
        compiler_params=_params(1),
        name="mlstm",
    )(qkvo, qkvo, qkvo, qkvo, gates_col, gates_row, norm_g.reshape(DEPTH, 1, ML_WIDTH))


def _mla_kernel(qi_ref, ki_ref, q_ref, k_ref, v_ref, o_ref, m_ref, acc_ref, *, tq, tk):
    p_id = pl.program_id(1)
    qi = qi_ref[p_id]
    ki = ki_ref[p_id]

    @pl.when(ki == 0)
    def _():
        m_ref[...] = jnp.full_like(m_ref, -jnp.inf)
        acc_ref[...] = jnp.zeros_like(acc_ref)

    last = (ki + 1) * tk >= (qi + 1) * tq
    heads = q_ref.shape[1] // MLA_QPAD

    def update(h, masked):
        qk = slice(h * MLA_QPAD, (h + 1) * MLA_QPAD)
        s = lax.dot_general(q_ref[:, qk], k_ref[:, qk], (((1,), (1,)), ((), ())), preferred_element_type=F32)
        if masked:
            q_chunk = (qi * tq + lax.broadcasted_iota(jnp.int32, (tq, tk), 0)) // CHUNK
            k_chunk = (ki * tk + lax.broadcasted_iota(jnp.int32, (tq, tk), 1)) // CHUNK
            s = jnp.where(k_chunk <= q_chunk, s, -jnp.inf)
        m_prev = m_ref[h]
        m_new = jnp.maximum(m_prev, jnp.max(s, axis=-1, keepdims=True))
        alpha = jnp.exp2(m_prev - m_new)
        p = jnp.exp2((s - m_new).astype(BF16))
        pv = jnp.dot(p, v_ref[:, h * MLA_VPAD:(h + 1) * MLA_VPAD], preferred_element_type=F32)
        acc_ref[h] = alpha * acc_ref[h] + pv
        m_ref[h] = m_new

    @pl.when(jnp.logical_not(last))
    def _():
        for h in range(heads):
            update(h, False)

    @pl.when(last)
    def _():
        for h in range(heads):
            update(h, True)
            o_ref[:, h * MLA_DV:(h + 1) * MLA_DV] = (acc_ref[h, :, :MLA_DV] / acc_ref[h, :, MLA_DV:]).astype(BF16)


def _mla_attention(q, k, v, *, tq=1024, tk=1024, heads=2):
    s = q.shape[0]
    assert tq % tk == 0 and tk % CHUNK == 0
    pairs = [(a, b) for a in range(s // tq) for b in range((a + 1) * tq // tk)]
    qi_tab = jnp.asarray([a for a, _ in pairs], jnp.int32)
    ki_tab = jnp.asarray([b for _, b in pairs], jnp.int32)
    grid_spec = pltpu.PrefetchScalarGridSpec(
        num_scalar_prefetch=2,
        grid=(MLA_HEADS // heads, len(pairs)),
        in_specs=[pl.BlockSpec((tq, heads * MLA_QPAD), lambda h, p, qt, kt: (qt[p], h)),
                  pl.BlockSpec((tk, heads * MLA_QPAD), lambda h, p, qt, kt: (kt[p], h)),
                  pl.BlockSpec((tk, heads * MLA_VPAD), lambda h, p, qt, kt: (kt[p], h))],
        out_specs=pl.BlockSpec((tq, heads * MLA_DV), lambda h, p, qt, kt: (qt[p], h)),
        scratch_shapes=[pltpu.VMEM((heads, tq, 1), F32), pltpu.VMEM((heads, tq, MLA_VPAD), F32)],
    )
    return pl.pallas_call(
        functools.partial(_mla_kernel, tq=tq, tk=tk),
        grid_spec=grid_spec,
        out_shape=jax.ShapeDtypeStruct((s, MLA_WIDTH), BF16),
        compiler_params=_params(2),
        name="mla_attn",
    )(qi_tab, ki_tab, q, k, v)


def _xattn_kernel(q_ref, k_ref, v_ref, o_ref):
    for h in range(X_HEADS):
        sl = slice(h * X_HEAD_DIM, (h + 1) * X_HEAD_DIM)
        s = lax.dot_general(q_ref[:, sl], k_ref[:, sl], (((1,), (1,)), ((), ())), preferred_element_type=F32)
        p = jnp.exp(s - jnp.max(s, axis=-1, keepdims=True))
        o = jnp.dot(p.astype(BF16), v_ref[:, sl], preferred_element_type=F32)
        o_ref[:, sl] = (o / jnp.sum(p, axis=-1, keepdims=True)).astype(BF16)


def _xattn(q, kv, *, tm=512):
    s = q.shape[0]
    n_mem = kv.shape[0]
    return pl.pallas_call(
        _xattn_kernel,
        grid=(s // tm,),
        in_specs=[pl.BlockSpec((tm, X_WIDTH), lambda i: (i, 0)),
                  pl.BlockSpec((n_mem, X_WIDTH), lambda i: (0, 0)),
                  pl.BlockSpec((n_mem, X_WIDTH), lambda i: (0, 1))],
        out_specs=pl.BlockSpec((tm, X_WIDTH), lambda i: (i, 0)),
        out_shape=jax.ShapeDtypeStruct((s, X_WIDTH), BF16),
        compiler_params=_params(1),
        name="xattn",
    )(q, kv, kv)


def _columns(cols):
    lane = lax.broadcasted_iota(jnp.int32, (cols[0].shape[0], len(cols)), 1)
    out = jnp.broadcast_to(cols[-1], lane.shape)
    for k in range(len(cols) - 2, -1, -1):
        out = jnp.where(lane == k, cols[k], out)
    return out


def _route_kernel(x_ref, w_ref, b_ref, idx_ref, gate_ref, rank_ref, cnt_ref, carry_ref):
    @pl.when(pl.program_id(0) == 0)
    def _():
        carry_ref[...] = jnp.zeros_like(carry_ref)

    logits = jnp.dot(x_ref[...], w_ref[...], preferred_element_type=F32, precision=lax.Precision.HIGHEST) + b_ref[...]
    tm = logits.shape[0]
    lane = lax.broadcasted_iota(jnp.int32, logits.shape, 1).astype(F32)
    vals, idxs, hots = [], [], []
    rest = logits
    for _ in range(TOP_K):
        mx = jnp.max(rest, axis=-1, keepdims=True)
        ix = jnp.min(jnp.where(rest == mx, lane, float(N_EXPERTS)), axis=-1, keepdims=True)
        hot = lane == ix
        rest = jnp.where(hot, -jnp.inf, rest)
        vals.append(mx)
        idxs.append(ix)
        hots.append(hot)
    exps = [jnp.exp(v - vals[0]) for v in vals]
    total = exps[0]
    for e in exps[1:]:
        total = total + e
    gate_ref[...] = _columns([e / total for e in exps])
    idx_ref[...] = _columns(idxs).astype(jnp.int32)
    picked = hots[0].astype(F32)
    for hot in hots[1:]:
        picked = picked + hot.astype(F32)
    row = lax.broadcasted_iota(jnp.int32, (tm, tm), 0)
    col = lax.broadcasted_iota(jnp.int32, (tm, tm), 1)
    before = jnp.dot((col < row).astype(BF16), picked.astype(BF16), preferred_element_type=F32) + carry_ref[...]
    ranks = [jnp.sum(jnp.where(hot, before, 0.0), axis=-1, keepdims=True) for hot in hots]
    rank_ref[...] = _columns(ranks).astype(jnp.int32)
    carry = carry_ref[...] + jnp.sum(picked, axis=0, keepdims=True)
    carry_ref[...] = carry
    cnt_ref[...] = carry.astype(jnp.int32)


def _route(x, w_router, b_router, layer, *, tm=512):
    m, d = x.shape
    tok = lambda w: pl.BlockSpec((tm, w), lambda i: (i, 0))
    return pl.pallas_call(
        _route_kernel,
        grid=(m // tm,),
        in_specs=[tok(d),
                  pl.BlockSpec((None, d, N_EXPERTS), lambda i: (layer, 0, 0)),
                  pl.BlockSpec((None, 1, N_EXPERTS), lambda i: (layer, 0, 0))],
        out_specs=[tok(TOP_K), tok(TOP_K), tok(TOP_K), pl.BlockSpec((1, N_EXPERTS), lambda i: (0, 0))],
        out_shape=[jax.ShapeDtypeStruct((m, TOP_K), jnp.int32), jax.ShapeDtypeStruct((m, TOP_K), F32),
                   jax.ShapeDtypeStruct((m, TOP_K), jnp.int32), jax.ShapeDtypeStruct((1, N_EXPERTS), jnp.int32)],
        scratch_shapes=[pltpu.VMEM((1, N_EXPERTS), F32)],
        compiler_params=_params(1),
        name="route",
    )(x, w_router, b_router.reshape(DEPTH, 1, N_EXPERTS))


def _meta_kernel(cnt_ref, idx_ref, rank_ref, dest_ref, be_ref, nu_ref, ps_ref):
    idx = idx_ref[...]
    dest = rank_ref[...]
    run = jnp.int32(0)
    ends = []
    for e in range(N_EXPERTS):
        ps_ref[e] = run
        dest = dest + jnp.where(idx == e, run, 0)
        run = run + ((cnt_ref[e] + (MOE_BLOCK - 1)) & -MOE_BLOCK)
        ends.append(run)
    dest_ref[...] = dest
    nu_ref[0] = lax.shift_right_logical(run, jnp.int32(MOE_BLOCK.bit_length() - 1))

    def body(b, carry):
        first_row = b * MOE_BLOCK
        passed = jnp.int32(0)
        for e in range(N_EXPERTS):
            passed = passed + (ends[e] <= first_row).astype(jnp.int32)
        be_ref[b] = jnp.minimum(passed, N_EXPERTS - 1)
        return carry

    lax.fori_loop(0, be_ref.shape[0], body, 0)


def _meta(counts, idx, rank):
    t = idx.shape[0]
    nb = t * TOP_K // MOE_BLOCK + N_EXPERTS
    smem = pl.BlockSpec(memory_space=pltpu.SMEM)
    vmem = pl.BlockSpec(memory_space=pltpu.VMEM)
    return pl.pallas_call(
        _meta_kernel,
        in_specs=[smem, vmem, vmem],
        out_specs=[vmem, smem, smem, smem],
        out_shape=[jax.ShapeDtypeStruct((t, TOP_K), jnp.int32), jax.ShapeDtypeStruct((nb,), jnp.int32),
                   jax.ShapeDtypeStruct((1,), jnp.int32), jax.ShapeDtypeStruct((N_EXPERTS,), jnp.int32)],
        name="moe_meta",
    )(counts.reshape(N_EXPERTS), idx, rank)


def _row_slab(ref, row):
    return ref.at[pl.ds(pl.multiple_of(row * SLAB, SLAB), SLAB)]


def _dispatch_kernel(dest_ref, cnt_ref, ps_ref, nu_ref, xs_ref, xg_ref, zero_ref, sem, zero_sem):
    i = pl.program_id(0)
    tokens = dest_ref.shape[0] // TOP_K

    @pl.when(i == 0)
    def _():
        zero_ref[...] = jnp.zeros_like(zero_ref)
        block_rows = MOE_BLOCK * SLAB
        n_blocks = xg_ref.shape[0] // block_rows

        def tail_copy(b):
            dst = xg_ref.at[pl.ds(pl.multiple_of(b * block_rows, block_rows), block_rows)]
            return pltpu.make_async_copy(zero_ref, dst, zero_sem)

        def start_tail(b, carry):
            tail_copy(b).start()
            return carry

        def wait_tail(b, carry):
            tail_copy(b).wait()
            return carry

        lax.fori_loop(nu_ref[0], n_blocks, start_tail, 0)
        lax.fori_loop(nu_ref[0], n_blocks, wait_tail, 0)

        def zero_copy(e):
            first = ps_ref[e] + (cnt_ref[e] & -MOE_BLOCK)
            dst = xg_ref.at[pl.ds(pl.multiple_of(first * SLAB, MOE_BLOCK * SLAB), MOE_BLOCK * SLAB)]
            return pltpu.make_async_copy(zero_ref, dst, zero_sem)

        for e in range(N_EXPERTS):
            @pl.when((cnt_ref[e] & (MOE_BLOCK - 1)) != 0)
            def _():
                zero_copy(e).start()
        for e in range(N_EXPERTS):
            @pl.when((cnt_ref[e] & (MOE_BLOCK - 1)) != 0)
            def _():
                zero_copy(e).wait()

    def body(t, carry):
        src = _row_slab(xs_ref, t)
        for k in range(TOP_K):
            pltpu.make_async_copy(src, _row_slab(xg_ref, dest_ref[t * TOP_K + k]), sem).start()
        return carry

    lax.fori_loop(0, tokens, body, 0)
    for _ in range(TOP_K):
        pltpu.make_async_copy(xs_ref, xg_ref.at[pl.ds(0, tokens * SLAB)], sem).wait()


def _dispatch(xs, dest_flat, counts, pad_starts, n_used, *, tokens=256):
    a = dest_flat.shape[0]
    rows = (a // MOE_BLOCK + N_EXPERTS) * MOE_BLOCK
    smem = pl.BlockSpec(memory_space=pltpu.SMEM)
    any_space = pl.BlockSpec(memory_space=pl.ANY)
    return pl.pallas_call(
        _dispatch_kernel,
        grid=(a // (tokens * TOP_K),),
        in_specs=[pl.BlockSpec((tokens * TOP_K,), lambda i: (i,), memory_space=pltpu.SMEM), smem, smem, smem,
                  pl.BlockSpec((tokens * SLAB, LANES), lambda i: (i, 0))],
        out_specs=any_space,
        out_shape=jax.ShapeDtypeStruct((rows * SLAB, LANES), jnp.uint32),
        scratch_shapes=[pltpu.VMEM((MOE_BLOCK * SLAB, LANES), jnp.uint32), pltpu.SemaphoreType.DMA,
                        pltpu.SemaphoreType.DMA],
        compiler_params=_params(1),
        name="moe_dispatch",
    )(dest_flat, counts.reshape(N_EXPERTS), pad_starts, n_used, xs)


def _expert_changed(be_ref, b):
    prev = be_ref[jnp.maximum(b - 1, 0)]
    return jnp.logical_or(b == 0, be_ref[b] != prev)


def _unpack_slabs(xp_ref, xb_ref):
    rows = xb_ref.shape[0]
    for j in range(SLAB):
        lo, hi = _unpack_slab_row(xp_ref, 0, rows, j)
        xb_ref[:, 2 * j * LANES:(2 * j + 1) * LANES] = lo.astype(BF16)
        xb_ref[:, (2 * j + 1) * LANES:(2 * j + 2) * LANES] = hi.astype(BF16)


def _gu_kernel(be_ref, nu_ref, xp_ref, wg_ref, wu_ref, bg_ref, bu_ref, h_ref, wgb_ref, wub_ref, xb_ref):
    b = pl.program_id(1)

    @pl.when(_expert_changed(be_ref, b))
    def _():
        _cast_weight(wg_ref, wgb_ref)
        _cast_weight(wu_ref, wub_ref)

    @pl.when(b < nu_ref[0])
    def _():
        _unpack_slabs(xp_ref, xb_ref)
        x = xb_ref[...]
        g = jnp.dot(x, wgb_ref[...], preferred_element_type=F32) + bg_ref[...]
        u = jnp.dot(x, wub_ref[...], preferred_element_type=F32) + bu_ref[...]
        g = jnp.minimum(g, SWIGLU_LIMIT)
        u = jnp.clip(u, -SWIGLU_LIMIT, SWIGLU_LIMIT)
        h_ref[...] = ((u + 1.0) * g * jax.nn.sigmoid(SWIGLU_ALPHA * g)).astype(BF16)

    @pl.when(b >= nu_ref[0])
    def _():
        h_ref[...] = jnp.zeros_like(h_ref)


def _expert_gu(xg, w_gu, b_gu, block_expert, n_used, layer, *, tf=256):
    p = xg.shape[0] // SLAB
    d = D_MODEL
    nb = p // MOE_BLOCK
    nt = D_EXPERT // tf
    live = lambda b, nu: jnp.minimum(b, nu[0] - 1)
    grid_spec = pltpu.PrefetchScalarGridSpec(
        num_scalar_prefetch=2,
        grid=(nt, nb),
        in_specs=[pl.BlockSpec((MOE_BLOCK * SLAB, LANES), lambda t, b, be, nu: (live(b, nu), 0)),
                  pl.BlockSpec((None, None, d, tf), lambda t, b, be, nu: (layer, be[b], 0, t)),
                  pl.BlockSpec((None, None, d, tf), lambda t, b, be, nu: (layer, be[b], 0, nt + t)),
                  pl.BlockSpec((None, None, 1, tf), lambda t, b, be, nu: (layer, be[b], 0, t)),
                  pl.BlockSpec((None, None, 1, tf), lambda t, b, be, nu: (layer, be[b], 0, nt + t))],
        out_specs=pl.BlockSpec((MOE_BLOCK, tf), lambda t, b, be, nu: (b, t)),
        scratch_shapes=[pltpu.VMEM((d, tf), BF16), pltpu.VMEM((d, tf), BF16), pltpu.VMEM((MOE_BLOCK, d), BF16)],
    )
    b_gu4 = b_gu.reshape(DEPTH, N_EXPERTS, 1, 2 * D_EXPERT)
    return pl.pallas_call(
        _gu_kernel,
        grid_spec=grid_spec,
        out_shape=jax.ShapeDtypeStruct((p, D_EXPERT), BF16),
        compiler_params=_params(2),
        name="expert_gu",
    )(block_expert, n_used, xg, w_gu, w_gu, b_gu4, b_gu4)


def _down_kernel(be_ref, nu_ref, h_ref, w_ref, bd_ref, y_ref, wb_ref):
    b = pl.program_id(0)

    @pl.when(_expert_changed(be_ref, b))
    def _():
        _cast_weight(w_ref, wb_ref, rows=256)

    @pl.when(b < nu_ref[0])
    def _():
        _pack_slabs(jnp.dot(h_ref[...], wb_ref[...], preferred_element_type=F32) + bd_ref[...], y_ref)

    @pl.when(b >= nu_ref[0])
    def _():
        y_ref[...] = jnp.zeros_like(y_ref)


def _expert_down(h, w_down, b_down, block_expert, n_used, layer):
    p, f = h.shape
    nb = p // MOE_BLOCK
    d = w_down.shape[-1]
    live = lambda b, nu: jnp.minimum(b, nu[0] - 1)
    grid_spec = pltpu.PrefetchScalarGridSpec(
        num_scalar_prefetch=2,
        grid=(nb,),
        in_specs=[pl.BlockSpec((MOE_BLOCK, f), lambda b, be, nu: (live(b, nu), 0)),
                  pl.BlockSpec((None, None, f, d), lambda b, be, nu: (layer, be[b], 0, 0)),
                  pl.BlockSpec((None, None, 1, d), lambda b, be, nu: (layer, be[b], 0, 0))],
        out_specs=pl.BlockSpec((MOE_BLOCK * SLAB, LANES), lambda b, be, nu: (b, 0)),
        scratch_shapes=[pltpu.VMEM((f, d), BF16)],
    )
    return pl.pallas_call(
        _down_kernel,
        grid_spec=grid_spec,
        out_shape=jax.ShapeDtypeStruct((p * SLAB, LANES), jnp.uint32),
        compiler_params=_params(1),
        name="expert_down",
    )(block_expert, n_used, h, w_down, b_down.reshape(DEPTH, N_EXPERTS, 1, d))


def _combine_ln_kernel(dcur_ref, dnxt_ref, x_ref, gate_ref, g_ref, b_ref, y_ref, o_ref, ob_ref, buf_ref, sem):
    i = pl.program_id(0)
    n = pl.num_programs(0)
    tm = x_ref.shape[0]
    per_tile = tm * TOP_K
    slot = i % 2

    def issue(dest_ref, tile, to_slot):
        base = (tile % (dest_ref.shape[0] // per_tile)) * per_tile

        def body(t, carry):
            for k in range(TOP_K):
                src = _row_slab(y_ref, dest_ref[base + t * TOP_K + k])
                pltpu.make_async_copy(src, _row_slab(buf_ref.at[to_slot], k * tm + t), sem.at[to_slot]).start()
            return carry

        lax.fori_loop(0, tm, body, 0)

    @pl.when(i == 0)
    def _():
        issue(dcur_ref, i, slot)

    @pl.when(i + 1 < n)
    def _():
        issue(dnxt_ref, i + 1, 1 - slot)

    pltpu.make_async_copy(y_ref.at[pl.ds(0, per_tile * SLAB)], buf_ref.at[slot], sem.at[slot]).wait()
    parts = []
    buf = buf_ref.at[slot]
    for j in range(SLAB):
        acc_lo = acc_hi = None
        for k in range(TOP_K):
            lo, hi = _unpack_slab_row(buf, k * tm * SLAB, tm, j)
            gate = gate_ref[:, k:k + 1]
            acc_lo = gate * lo if acc_lo is None else acc_lo + gate * lo
            acc_hi = gate * hi if acc_hi is None else acc_hi + gate * hi
        parts.extend([acc_lo, acc_hi])
    v = DN_ALPHA * x_ref[...] + jnp.concatenate(parts, axis=1)
    y = _ln_rows(v, g_ref[...], b_ref[...])
    o_ref[...] = y
    ob_ref[...] = y.astype(BF16)


def _combine_ln(x, y_slabs, dest_flat, gates, g, b, layer, *, tm=128, dest_block=1024):
    m, d = x.shape
    tiles_per_block = dest_block // (tm * TOP_K)
    last = m * TOP_K // dest_block - 1
    row = pl.BlockSpec((tm, d), lambda i: (i, 0))
    par = pl.BlockSpec((None, 1, d), lambda i: (layer, 0, 0))
    return pl.pallas_call(
        _combine_ln_kernel,
        grid=(m // tm,),
        in_specs=[pl.BlockSpec((dest_block,), lambda i: (i // tiles_per_block,), memory_space=pltpu.SMEM),
                  pl.BlockSpec((dest_block,), lambda i: (jnp.minimum((i + 1) // tiles_per_block, last),),
                               memory_space=pltpu.SMEM),
                  row, pl.BlockSpec((tm, TOP_K), lambda i: (i, 0)), par, par,
                  pl.BlockSpec(memory_space=pl.ANY)],
        out_specs=[row, row],
        out_shape=[jax.ShapeDtypeStruct((m, d), F32), jax.ShapeDtypeStruct((m, d), BF16)],
        scratch_shapes=[pltpu.VMEM((2, tm * TOP_K * SLAB, LANES), jnp.uint32), pltpu.SemaphoreType.DMA((2,))],
        compiler_params=_params(1),
        name="combine_ln",
    )(dest_flat, dest_flat, x, gates, g.reshape(DEPTH, 1, d), b.reshape(DEPTH, 1, d), y_slabs)


def _swap_halves(w):
    half = w.shape[-1] // 2
    return jnp.concatenate([w[..., half:], w[..., :half]], axis=-1)


def kernel(x, mem, positions, w_in, ml_b_i, ml_b_f, ml_norm_g, mla_g_q, mla_w_uq, mla_g_kv, mla_w_ukv, w_out, ln1_g, ln1_b, x_g_mem, x_b_mem, x_w_q, x_w_kv, x_w_o, ln2_g, ln2_b, w_router, b_router, w_gu, b_gu, w_down, b_down, ln3_g, ln3_b):
    batch, seq, d = x.shape
    assert batch == 1
    t = batch * seq
    x = x.reshape(t, d)
    mem2 = mem.reshape(-1, d)
    pos = positions.reshape(t, 1)

    kr_w = w_in[:, :, KR_OFF:KR_OFF + MLA_ROPE]
    w_aux = jnp.concatenate(
        [w_in[:, :, CQ_OFF:KR_OFF], kr_w, _swap_halves(kr_w), w_in[:, :, GATES_OFF:CQ_OFF],
         jnp.zeros((DEPTH, d, AUX_WIDTH - (Q_LORA + KV_LORA + 2 * MLA_ROPE + 2 * ML_HEADS)), w_in.dtype)], axis=-1)
    uq = mla_w_uq.reshape(DEPTH, Q_LORA, MLA_HEADS, MLA_QK)
    uq_rope = uq[..., MLA_NOPE:]
    w_uq = jnp.concatenate([uq[..., :MLA_NOPE], uq_rope, _swap_halves(uq_rope)], axis=-1)
    w_uq = w_uq.reshape(DEPTH, Q_LORA, MLA_HEADS * MLA_QPAD)
    gate_bias = jnp.concatenate([ml_b_i, ml_b_f, jnp.zeros((DEPTH, LANES - 2 * ML_HEADS), F32)], axis=-1)
    gate_bias = gate_bias.reshape(DEPTH, 1, LANES)
    half = MLA_ROPE // 2
    inv = ROPE_THETA ** (-jnp.arange(half, dtype=F32) / half)
    inv = jnp.tile(inv, 4).reshape(1, LANES)
    sign = jnp.concatenate([jnp.ones((2 * half,), F32), -jnp.ones((half,), F32), jnp.ones((half,), F32)])
    sign = sign.reshape(1, LANES)

    xb = x.astype(BF16)
    for l in range(DEPTH):
        qkvo = _matmul([xb], w_in, l, tm=1024, tn=512, n_out=QKVO_WIDTH, name="in_proj")
        aux = _matmul([xb], w_aux, l, tm=1024, tn=256, n_out=AUX_WIDTH, out_dtype=F32, name="aux_proj")
        cqn, ckvn, krope, cs, gates = _mixer_prep(aux, pos, inv, sign, mla_g_q, mla_g_kv, gate_bias, l)
        y_ml = _mlstm(qkvo, gates, gates.T, ml_norm_g, l)
        q = _matmul([cqn], w_uq, l, tm=1024, tn=512, n_out=MLA_HEADS * MLA_QPAD, epilogue=_q_epilogue,
                    extra=(cs,), extra_specs=(pl.BlockSpec((1024, LANES), lambda j, i: (i, 0)),), name="q_up")
        k, v = _kv_up(ckvn, mla_w_ukv, krope, l)
        y_mla = _mla_attention(q, k, v)
        h = _matmul([y_ml, y_mla], w_out, l, tm=1024, tn=512, n_out=d, name="out_proj")
        x, xb = _res_ln(x, h, ln1_g, ln1_b, l)
        mem_n = _ln_bf16(mem2, x_g_mem, x_b_mem, l)
        kv = _matmul([mem_n], x_w_kv, l, tm=mem_n.shape[0], tn=512, n_out=2 * X_WIDTH, name="mem_kv")
        qx = _matmul([xb], x_w_q, l, tm=1024, tn=512, n_out=X_WIDTH,
                     epilogue=lambda acc: acc * (X_HEAD_DIM ** -0.5), name="xq_proj")
        ox = _xattn(qx, kv)
        h = _matmul([ox], x_w_o, l, tm=1024, tn=512, n_out=d, name="xo_proj")
        x, x_slabs = _res_ln_slab(x, h, ln2_g, ln2_b, l)
        top_idx, gates_k, rank, counts = _route(x, w_router, b_router, l)
        dest, block_expert, n_used, pad_starts = _meta(counts, top_idx, rank)
        dest_flat = dest.reshape(t * TOP_K)
        xg = _dispatch(x_slabs, dest_flat, counts, pad_starts, n_used)
        hmid = _expert_gu(xg, w_gu, b_gu, block_expert, n_used, l)
        y_slabs = _expert_down(hmid, w_down, b_down, block_expert, n_used, l)
        x, xb = _combine_ln(x, y_slabs, dest_flat, gates_k, ln3_g, ln3_b, l)
    return x.reshape(batch, seq, d)
```

```python
import functools

import jax
import jax.numpy as jnp
from jax import lax
from jax.experimental import pallas as pl
from jax.experimental.pallas import tpu as pltpu

D_MODEL = 4096
DEPTH = 2
CHUNK = 64
ML_HEADS = 8
ML_DQK = 128
ML_DV = 256
ML_WIDTH = ML_HEADS * ML_DV
GATE_SOFTCAP = 15.0
MLA_HEADS = 16
MLA_NOPE = 128
MLA_ROPE = 64
MLA_DV = 128
MLA_QK = MLA_NOPE + MLA_ROPE
MLA_WIDTH = MLA_HEADS * MLA_DV
MLA_QPAD = 256
MLA_VPAD = 256
LOG2_E = 1.4426950408889634
Q_LORA = 1024
KV_LORA = 512
ROPE_THETA = 10000.0
X_HEADS = 4
X_HEAD_DIM = 256
X_WIDTH = X_HEADS * X_HEAD_DIM
N_EXPERTS = 32
TOP_K = 4
D_EXPERT = 768
SWIGLU_LIMIT = 7.0
SWIGLU_ALPHA = 1.702
DN_ALPHA = (2.0 * DEPTH) ** 0.25
LN_EPS = 1e-5
RMS_EPS = 1e-6
QKVO_WIDTH = 2 * ML_HEADS * ML_DQK + 2 * ML_WIDTH
GATES_OFF = QKVO_WIDTH
CQ_OFF = GATES_OFF + 2 * ML_HEADS
CKV_OFF = CQ_OFF + Q_LORA
KR_OFF = CKV_OFF + KV_LORA
AUX_WIDTH = 1792

VMEM_LIMIT = 56 * 1024 * 1024
MOE_BLOCK = 256
LANES = 128
SLAB = D_MODEL // 2 // LANES
ML_CHUNK = 128

BF16 = jnp.bfloat16
F32 = jnp.float32


def _params(n_axes, vmem=VMEM_LIMIT):
    return pltpu.CompilerParams(dimension_semantics=("arbitrary",) * n_axes, vmem_limit_bytes=vmem)


def _cast_weight(w_ref, wb_ref, rows=512):
    k = w_ref.shape[0]
    rows = min(rows, k)

    def body(c, carry):
        sl = pl.ds(pl.multiple_of(c * rows, rows), rows)
        wb_ref[sl, :] = w_ref[sl, :].astype(BF16)
        return carry

    lax.fori_loop(0, k // rows, body, 0)


def _cast_weight_transposed(wt_ref, wb_ref, cols=512):
    k = wt_ref.shape[1]
    for c in range(k // cols):
        wb_ref[c * cols:(c + 1) * cols, :] = wt_ref[:, c * cols:(c + 1) * cols].T.astype(BF16)


def _mm_kernel(*refs, n_x, epilogue, w_transposed):
    x_refs = refs[:n_x]
    w_ref = refs[n_x]
    extra = refs[n_x + 1:-2]
    o_ref, wb_ref = refs[-2:]

    @pl.when(pl.program_id(1) == 0)
    def _():
        if w_transposed:
            _cast_weight_transposed(w_ref, wb_ref)
        else:
            _cast_weight(w_ref, wb_ref)

    off = 0
    acc = None
    for x_ref in x_refs:
        kx = x_ref.shape[1]
        part = jnp.dot(x_ref[...], wb_ref[off:off + kx, :], preferred_element_type=F32)
        acc = part if acc is None else acc + part
        off += kx
    if epilogue is not None:
        acc = epilogue(acc, *extra)
    o_ref[...] = acc.astype(o_ref.dtype)


def _matmul(xs, w, layer, *, tm, tn, n_out, col_off=0, out_dtype=BF16, epilogue=None, extra=(), extra_specs=(),
            w_transposed=False, name="mm"):
    m = xs[0].shape[0]
    k = sum(x.shape[1] for x in xs)
    assert w.shape[2 if w_transposed else 1] == k and m % tm == 0 and n_out % tn == 0 and col_off % tn == 0
    cb = col_off // tn
    in_specs = [pl.BlockSpec((tm, x.shape[1]), lambda j, i: (i, 0)) for x in xs]
    if w_transposed:
        in_specs.append(pl.BlockSpec((None, tn, k), lambda j, i: (layer, j + cb, 0)))
    else:
        in_specs.append(pl.BlockSpec((None, k, tn), lambda j, i: (layer, 0, j + cb)))
    in_specs.extend(extra_specs)
    return pl.pallas_call(
        functools.partial(_mm_kernel, n_x=len(xs), epilogue=epilogue, w_transposed=w_transposed),
        grid=(n_out // tn, m // tm),
        in_specs=in_specs,
        out_specs=pl.BlockSpec((tm, tn), lambda j, i: (i, j)),
        out_shape=jax.ShapeDtypeStruct((m, n_out), out_dtype),
        scratch_shapes=[pltpu.VMEM((k, tn), BF16)],
        compiler_params=_params(2),
        name=name,
    )(*xs, w, *extra)


def _ln_rows(v, g, b):
    mu = jnp.mean(v, axis=-1, keepdims=True)
    c = v - mu
    var = jnp.mean(c * c, axis=-1, keepdims=True)
    return c * lax.rsqrt(var + LN_EPS) * g + b


def _res_ln_kernel(x_ref, h_ref, g_ref, b_ref, o_ref, ob_ref):
    v = DN_ALPHA * x_ref[...] + h_ref[...].astype(F32)
    y = _ln_rows(v, g_ref[...], b_ref[...])
    o_ref[...] = y
    ob_ref[...] = y.astype(BF16)


def _res_ln(x, h, g, b, layer, *, tm=256):
    m, d = x.shape
    row = pl.BlockSpec((tm, d), lambda i: (i, 0))
    par = pl.BlockSpec((None, 1, d), lambda i: (layer, 0, 0))
    return pl.pallas_call(
        _res_ln_kernel,
        grid=(m // tm,),
        in_specs=[row, row, par, par],
        out_specs=[row, row],
        out_shape=[jax.ShapeDtypeStruct((m, d), F32), jax.ShapeDtypeStruct((m, d), BF16)],
        compiler_params=_params(1),
        name="res_ln",
    )(x, h, g.reshape(DEPTH, 1, d), b.reshape(DEPTH, 1, d))


def _pack_slabs(y, slab_ref):
    rows = y.shape[0]
    for j in range(SLAB):
        lo = y[:, 2 * j * LANES:(2 * j + 1) * LANES]
        hi = y[:, (2 * j + 1) * LANES:(2 * j + 2) * LANES]
        slab_ref[pl.ds(j, rows, stride=SLAB), :] = pltpu.pack_elementwise([lo, hi], packed_dtype=BF16)


def _unpack_slab_row(slab_ref, first, rows, j):
    u = slab_ref[pl.ds(first + j, rows, stride=SLAB), :]
    lo = pltpu.unpack_elementwise(u, index=0, packed_dtype=BF16, unpacked_dtype=F32)
    hi = pltpu.unpack_elementwise(u, index=1, packed_dtype=BF16, unpacked_dtype=F32)
    return lo, hi


def _res_ln_slab_kernel(x_ref, h_ref, g_ref, b_ref, o_ref, xp_ref):
    v = DN_ALPHA * x_ref[...] + h_ref[...].astype(F32)
    y = _ln_rows(v, g_ref[...], b_ref[...])
    o_ref[...] = y
    _pack_slabs(y, xp_ref)


def _res_ln_slab(x, h, g, b, layer, *, tm=256):
    m, d = x.shape
    row = pl.BlockSpec((tm, d), lambda i: (i, 0))
    par = pl.BlockSpec((None, 1, d), lambda i: (layer, 0, 0))
    return pl.pallas_call(
        _res_ln_slab_kernel,
        grid=(m // tm,),
        in_specs=[row, row, par, par],
        out_specs=[row, pl.BlockSpec((tm * SLAB, LANES), lambda i: (i, 0))],
        out_shape=[jax.ShapeDtypeStruct((m, d), F32), jax.ShapeDtypeStruct((m * SLAB, LANES), jnp.uint32)],
        compiler_params=_params(1),
        name="res_ln_slab",
    )(x, h, g.reshape(DEPTH, 1, d), b.reshape(DEPTH, 1, d))


def _ln_kernel(x_ref, g_ref, b_ref, ob_ref):
    ob_ref[...] = _ln_rows(x_ref[...], g_ref[...], b_ref[...]).astype(BF16)


def _ln_bf16(x, g, b, layer, *, tm=256):
    m, d = x.shape
    row = pl.BlockSpec((tm, d), lambda i: (i, 0))
    par = pl.BlockSpec((None, 1, d), lambda i: (layer, 0, 0))
    return pl.pallas_call(
        _ln_kernel,
        grid=(m // tm,),
        in_specs=[row, par, par],
        out_specs=row,
        out_shape=jax.ShapeDtypeStruct((m, d), BF16),
        compiler_params=_params(1),
        name="mem_ln",
    )(x, g.reshape(DEPTH, 1, d), b.reshape(DEPTH, 1, d))


def _prep_kernel(cq_ref, ckv_ref, misc_ref, pos_ref, inv_ref, sign_ref, gq_ref, gkv_ref, bias_ref,
                 cqn_ref, ckvn_ref, krope_ref, cs_ref, gates_ref):
    def rms(v, g):
        return v * lax.rsqrt(jnp.mean(v * v, axis=-1, keepdims=True) + RMS_EPS) * g

    cqn_ref[...] = rms(cq_ref[...], gq_ref[...]).astype(BF16)
    ckvn_ref[...] = rms(ckv_ref[...], gkv_ref[...]).astype(BF16)
    ang = pos_ref[...].astype(F32) * inv_ref[...]
    lane = lax.broadcasted_iota(jnp.int32, ang.shape, 1)
    cs = jnp.where(lane < MLA_ROPE, jnp.cos(ang), jnp.sin(ang) * sign_ref[...])
    cs_ref[...] = cs
    misc = misc_ref[...]
    t = misc[:, :LANES] * cs
    rot = t + pltpu.roll(t, MLA_ROPE, axis=1)
    krope_ref[...] = jnp.where(lane < MLA_ROPE, rot, 0.0).astype(BF16)
    sc = GATE_SOFTCAP * jnp.tanh((misc[:, LANES:] + bias_ref[...]) / GATE_SOFTCAP)
    log_sig = jnp.minimum(sc, 0.0) - jnp.log(1.0 + jnp.exp(-jnp.abs(sc)))
    gates_ref[...] = jnp.where(lane < ML_HEADS, sc, log_sig)


def _mixer_prep(aux, pos, inv, sign, g_q, g_kv, bias, layer, *, tm=512):
    m = aux.shape[0]
    row = lambda w, c: pl.BlockSpec((tm, w), lambda i: (i, c))
    const = lambda w: pl.BlockSpec((1, w), lambda i: (0, 0))
    par = lambda w: pl.BlockSpec((None, 1, w), lambda i: (layer, 0, 0))
    return pl.pallas_call(
        _prep_kernel,
        grid=(m // tm,),
        in_specs=[row(Q_LORA, 0), row(KV_LORA, Q_LORA // KV_LORA), row(2 * LANES, (Q_LORA + KV_LORA) // (2 * LANES)),
                  pl.BlockSpec((tm, 1), lambda i: (i, 0)), const(LANES), const(LANES),
                  par(Q_LORA), par(KV_LORA), par(LANES)],
        out_specs=[row(Q_LORA, 0), row(KV_LORA, 0), row(LANES, 0), row(LANES, 0), row(LANES, 0)],
        out_shape=[jax.ShapeDtypeStruct((m, Q_LORA), BF16), jax.ShapeDtypeStruct((m, KV_LORA), BF16),
                   jax.ShapeDtypeStruct((m, LANES), BF16), jax.ShapeDtypeStruct((m, LANES), F32),
                   jax.ShapeDtypeStruct((m, LANES), F32)],
        compiler_params=_params(1),
        name="mixer_prep",
    )(aux, aux, aux, pos, inv, sign, g_q.reshape(DEPTH, 1, Q_LORA), g_kv.reshape(DEPTH, 1, KV_LORA), bias)


def _q_epilogue(acc, cs_ref):
    scale = MLA_QK ** -0.5 * LOG2_E
    cs = cs_ref[...]
    outs = []
    for h in range(acc.shape[1] // MLA_QPAD):
        base = h * MLA_QPAD
        outs.append(acc[:, base:base + MLA_NOPE] * scale)
        u = acc[:, base + MLA_NOPE:base + MLA_QPAD] * cs
        outs.append((u + pltpu.roll(u, MLA_ROPE, axis=1)) * scale)
    return jnp.concatenate(outs, axis=1)


def _kv_kernel(x_ref, w_ref, krope_ref, k_ref, v_ref, wb_ref):
    @pl.when(pl.program_id(1) == 0)
    def _():
        _cast_weight(w_ref, wb_ref)

    r = jnp.dot(x_ref[...], wb_ref[...], preferred_element_type=F32)
    kr = krope_ref[...]
    per_head = MLA_NOPE + MLA_DV
    for h in range(r.shape[1] // per_head):
        k_ref[:, h * MLA_QPAD:h * MLA_QPAD + MLA_NOPE] = r[:, h * per_head:h * per_head + MLA_NOPE].astype(BF16)
        k_ref[:, h * MLA_QPAD + MLA_NOPE:(h + 1) * MLA_QPAD] = kr
        v_ref[:, h * MLA_VPAD:h * MLA_VPAD + MLA_DV] = r[:, h * per_head + MLA_NOPE:(h + 1) * per_head].astype(BF16)
        v_ref[:, h * MLA_VPAD + MLA_DV:(h + 1) * MLA_VPAD] = jnp.ones((r.shape[0], MLA_VPAD - MLA_DV), BF16)


def _kv_up(ckvn, w_ukv, krope, layer, *, tm=1024, heads=2):
    m = ckvn.shape[0]
    tn = heads * (MLA_NOPE + MLA_DV)
    return pl.pallas_call(
        _kv_kernel,
        grid=(MLA_HEADS // heads, m // tm),
        in_specs=[pl.BlockSpec((tm, KV_LORA), lambda j, i: (i, 0)),
                  pl.BlockSpec((None, KV_LORA, tn), lambda j, i: (layer, 0, j)),
                  pl.BlockSpec((tm, LANES), lambda j, i: (i, 0))],
        out_specs=[pl.BlockSpec((tm, heads * MLA_QPAD), lambda j, i: (i, j)),
                   pl.BlockSpec((tm, heads * MLA_VPAD), lambda j, i: (i, j))],
        out_shape=[jax.ShapeDtypeStruct((m, MLA_HEADS * MLA_QPAD), BF16),
                   jax.ShapeDtypeStruct((m, MLA_HEADS * MLA_VPAD), BF16)],
        scratch_shapes=[pltpu.VMEM((KV_LORA, tn), BF16)],
        compiler_params=_params(2),
        name="kv_up",
    )(ckvn, w_ukv, krope)


def _mlstm_kernel(q_ref, k_ref, v_ref, o_ref, gc_ref, gr_ref, ng_ref, y_ref, c_ref, n_ref, m_ref):
    L = q_ref.shape[0]
    scale = ML_DQK ** -0.5

    @pl.when(pl.program_id(0) == 0)
    def _():
        c_ref[...] = jnp.zeros_like(c_ref)
        n_ref[...] = jnp.zeros_like(n_ref)
        m_ref[...] = jnp.zeros_like(m_ref)

    gc = gc_ref[...]
    gr = gr_ref[...]
    row = lax.broadcasted_iota(jnp.int32, (L, L), 0)
    col = lax.broadcasted_iota(jnp.int32, (L, L), 1)
    causal = col <= row
    lower = causal.astype(F32)
    upper = (row <= col).astype(F32)
    b_col = jnp.dot(lower, gc, preferred_element_type=F32, precision=lax.Precision.HIGHEST)
    b_row = jnp.dot(gr, upper, preferred_element_type=F32, precision=lax.Precision.HIGHEST)

    for h in range(ML_HEADS):
        bc = b_col[:, ML_HEADS + h:ML_HEADS + h + 1]
        br = b_row[ML_HEADS + h:ML_HEADS + h + 1, :]
        lic = gc[:, h:h + 1]
        lir = gr[h:h + 1, :]
        m_prev = m_ref[h]
        d = jnp.where(causal, bc - br + lir, -jnp.inf)
        a_inter = bc + m_prev
        m_t = jnp.maximum(a_inter, jnp.max(d, axis=-1, keepdims=True))
        w_inter = jnp.exp(a_inter - m_t)
        qh = q_ref[:, h * ML_DQK:(h + 1) * ML_DQK]
        kh = k_ref[:, h * ML_DQK:(h + 1) * ML_DQK]
        vh = v_ref[:, h * ML_DV:(h + 1) * ML_DV]
        s = lax.dot_general(qh, kh, (((1,), (1,)), ((), ())), preferred_element_type=F32)
        p = s * scale * jnp.exp(d - m_t)
        c_prev = c_ref[h]
        n_prev = n_ref[h]
        num = w_inter * jnp.dot(qh, c_prev.astype(BF16), preferred_element_type=F32)
        num = num + jnp.dot(p.astype(BF16), vh, preferred_element_type=F32)
        den = w_inter * jnp.sum(qh.astype(F32) * n_prev, axis=-1, keepdims=True) + jnp.sum(p, axis=-1, keepdims=True)
        hout = num / jnp.maximum(jnp.abs(den), jnp.exp(-m_t))
        m_new = m_t[L - 1:L, :]
        b_last = bc[L - 1:L, :]
        w_prev = jnp.exp(b_last + m_prev - m_new)
        ws_col = jnp.exp(b_last - bc + lic - m_new) * scale
        kw = kh.astype(F32) * ws_col
        c_ref[h] = w_prev * c_prev + lax.dot_general(kw.astype(BF16), vh, (((0,), (0,)), ((), ())),
                                                     preferred_element_type=F32)
        n_ref[h] = w_prev * n_prev + jnp.sum(kw, axis=0, keepdims=True)
        m_ref[h] = m_new
        hn = hout * lax.rsqrt(jnp.mean(hout * hout, axis=-1, keepdims=True) + RMS_EPS)
        hn = hn * ng_ref[:, h * ML_DV:(h + 1) * ML_DV]
        gate = jax.nn.sigmoid(o_ref[:, h * ML_DV:(h + 1) * ML_DV].astype(F32))
        y_ref[:, h * ML_DV:(h + 1) * ML_DV] = (hn * gate).astype(BF16)


def _mlstm(qkvo, gates_col, gates_row, norm_g, layer, *, chunk=ML_CHUNK):
    s = qkvo.shape[0]
    hq = ML_HEADS * ML_DQK
    return pl.pallas_call(
        _mlstm_kernel,
        grid=(s // chunk,),
        in_specs=[pl.BlockSpec((chunk, hq), lambda c: (c, 0)),
                  pl.BlockSpec((chunk, hq), lambda c: (c, 1)),
                  pl.BlockSpec((chunk, ML_WIDTH), lambda c: (c, 2 * hq // ML_WIDTH)),
                  pl.BlockSpec((chunk, ML_WIDTH), lambda c: (c, 2 * hq // ML_WIDTH + 1)),
                  pl.BlockSpec((chunk, LANES), lambda c: (c, 0)),
                  pl.BlockSpec((LANES, chunk), lambda c: (0, c)),
                  pl.BlockSpec((None, 1, ML_WIDTH), lambda c: (layer, 0, 0))],
        out_specs=pl.BlockSpec((chunk, ML_WIDTH), lambda c: (c, 0)),
        out_shape=jax.ShapeDtypeStruct((s, ML_WIDTH), BF16),
        scratch_shapes=[pltpu.VMEM((ML_HEADS, ML_DQK, ML_DV), F32), pltpu.VMEM((ML_HEADS, 1, ML_DQK), F32),
                        pltpu.VMEM((ML_HEADS, 1, 1), F32)],
        compiler_params=_params(1),
        name="mlstm",
    )(qkvo, qkvo, qkvo, qkvo, gates_col, gates_row, norm_g.reshape(DEPTH, 1, ML_WIDTH))


def _mla_kernel(qi_ref, ki_ref, q_ref, k_ref, v_ref, o_ref, m_ref, acc_ref, *, tq, tk):
    p_id = pl.program_id(1)
    qi = qi_ref[p_id]
    ki = ki_ref[p_id]

    @pl.when(ki == 0)
    def _():
        m_ref[...] = jnp.full_like(m_ref, -jnp.inf)
        acc_ref[...] = jnp.zeros_like(acc_ref)

    s = lax.dot_general(q_ref[...], k_ref[...], (((1,), (1,)), ((), ())), preferred_element_type=F32)
    last = (ki + 1) * tk >= (qi + 1) * tq

    def update(s):
        m_prev = m_ref[...]
        m_new = jnp.maximum(m_prev, jnp.max(s, axis=-1, keepdims=True))
        alpha = jnp.exp2(m_prev - m_new)
        p = jnp.exp2((s - m_new).astype(BF16))
        acc_ref[...] = alpha * acc_ref[...] + jnp.dot(p, v_ref[...], preferred_element_type=F32)
        m_ref[...] = m_new

    @pl.when(jnp.logical_not(last))
    def _():
        update(s)

    @pl.when(last)
    def _():
        q_chunk = (qi * tq + lax.broadcasted_iota(jnp.int32, (tq, tk), 0)) // CHUNK
        k_chunk = (ki * tk + lax.broadcasted_iota(jnp.int32, (tq, tk), 1)) // CHUNK
        update(jnp.where(k_chunk <= q_chunk, s, -jnp.inf))
        o_ref[...] = (acc_ref[:, :MLA_DV] / acc_ref[:, MLA_DV:]).astype(BF16)


def _mla_attention(q, k, v, *, tq=1024, tk=1024):
    s = q.shape[0]
    assert tq % tk == 0 and tk % CHUNK == 0
    pairs = [(a, b) for a in range(s // tq) for b in range((a + 1) * tq // tk)]
    qi_tab = jnp.asarray([a for a, _ in pairs], jnp.int32)
    ki_tab = jnp.asarray([b for _, b in pairs], jnp.int32)
    grid_spec = pltpu.PrefetchScalarGridSpec(
        num_scalar_prefetch=2,
        grid=(MLA_HEADS, len(pairs)),
        in_specs=[pl.BlockSpec((tq, MLA_QPAD), lambda h, p, qt, kt: (qt[p], h)),
                  pl.BlockSpec((tk, MLA_QPAD), lambda h, p, qt, kt: (kt[p], h)),
                  pl.BlockSpec((tk, MLA_VPAD), lambda h, p, qt, kt: (kt[p], h))],
        out_specs=pl.BlockSpec((tq, MLA_DV), lambda h, p, qt, kt: (qt[p], h)),
        scratch_shapes=[pltpu.VMEM((tq, 1), F32), pltpu.VMEM((tq, MLA_VPAD), F32)],
    )
    return pl.pallas_call(
        functools.partial(_mla_kernel, tq=tq, tk=tk),
        grid_spec=grid_spec,
        out_shape=jax.ShapeDtypeStruct((s, MLA_WIDTH), BF16),
        compiler_params=_params(2),
        name="mla_attn",
    )(qi_tab, ki_tab, q, k, v)


def _xattn_kernel(q_ref, k_ref, v_ref, o_ref):
    for h in range(X_HEADS):
        sl = slice(h * X_HEAD_DIM, (h + 1) * X_HEAD_DIM)
        s = lax.dot_general(q_ref[:, sl], k_ref[:, sl], (((1,), (1,)), ((), ())), preferred_element_type=F32)
        p = jnp.exp(s - jnp.max(s, axis=-1, keepdims=True))
        o = jnp.dot(p.astype(BF16), v_ref[:, sl], preferred_element_type=F32)
        o_ref[:, sl] = (o / jnp.sum(p, axis=-1, keepdims=True)).astype(BF16)


def _xattn(q, kv, *, tm=512):
    s = q.shape[0]
    n_mem = kv.shape[0]
    return pl.pallas_call(
        _xattn_kernel,
        grid=(s // tm,),
        in_specs=[pl.BlockSpec((tm, X_WIDTH), lambda i: (i, 0)),
                  pl.BlockSpec((n_mem, X_WIDTH), lambda i: (0, 0)),
                  pl.BlockSpec((n_mem, X_WIDTH), lambda i: (0, 1))],
        out_specs=pl.BlockSpec((tm, X_WIDTH), lambda i: (i, 0)),
        out_shape=jax.ShapeDtypeStruct((s, X_WIDTH), BF16),
        compiler_params=_params(1),
        name="xattn",
    )(q, kv, kv)


def _columns(cols):
    lane = lax.broadcasted_iota(jnp.int32, (cols[0].shape[0], len(cols)), 1)
    out = jnp.broadcast_to(cols[-1], lane.shape)
    for k in range(len(cols) - 2, -1, -1):
        out = jnp.where(lane == k, cols[k], out)
    return out


def _route_kernel(x_ref, w_ref, b_ref, idx_ref, gate_ref, rank_ref, cnt_ref, carry_ref):
    @pl.when(pl.program_id(0) == 0)
    def _():
        carry_ref[...] = jnp.zeros_like(carry_ref)

    logits = jnp.dot(x_ref[...], w_ref[...], preferred_element_type=F32, precision=lax.Precision.HIGHEST) + b_ref[...]
    tm = logits.shape[0]
    lane = lax.broadcasted_iota(jnp.int32, logits.shape, 1).astype(F32)
    vals, idxs, hots = [], [], []
    rest = logits
    for _ in range(TOP_K):
        mx = jnp.max(rest, axis=-1, keepdims=True)
        ix = jnp.min(jnp.where(rest == mx, lane, float(N_EXPERTS)), axis=-1, keepdims=True)
        hot = lane == ix
        rest = jnp.where(hot, -jnp.inf, rest)
        vals.append(mx)
        idxs.append(ix)
        hots.append(hot)
    exps = [jnp.exp(v - vals[0]) for v in vals]
    total = exps[0]
    for e in exps[1:]:
        total = total + e
    gate_ref[...] = _columns([e / total for e in exps])
    idx_ref[...] = _columns(idxs).astype(jnp.int32)
    picked = hots[0].astype(F32)
    for hot in hots[1:]:
        picked = picked + hot.astype(F32)
    row = lax.broadcasted_iota(jnp.int32, (tm, tm), 0)
    col = lax.broadcasted_iota(jnp.int32, (tm, tm), 1)
    before = jnp.dot((col < row).astype(BF16), picked.astype(BF16), preferred_element_type=F32) + carry_ref[...]
    ranks = [jnp.sum(jnp.where(hot, before, 0.0), axis=-1, keepdims=True) for hot in hots]
    rank_ref[...] = _columns(ranks).astype(jnp.int32)
    carry = carry_ref[...] + jnp.sum(picked, axis=0, keepdims=True)
    carry_ref[...] = carry
    cnt_ref[...] = carry.astype(jnp.int32)


def _route(x, w_router, b_router, layer, *, tm=512):
    m, d = x.shape
    tok = lambda w: pl.BlockSpec((tm, w), lambda i: (i, 0))
    return pl.pallas_call(
        _route_kernel,
        grid=(m // tm,),
        in_specs=[tok(d),
                  pl.BlockSpec((None, d, N_EXPERTS), lambda i: (layer, 0, 0)),
                  pl.BlockSpec((None, 1, N_EXPERTS), lambda i: (layer, 0, 0))],
        out_specs=[tok(TOP_K), tok(TOP_K), tok(TOP_K), pl.BlockSpec((1, N_EXPERTS), lambda i: (0, 0))],
        out_shape=[jax.ShapeDtypeStruct((m, TOP_K), jnp.int32), jax.ShapeDtypeStruct((m, TOP_K), F32),
                   jax.ShapeDtypeStruct((m, TOP_K), jnp.int32), jax.ShapeDtypeStruct((1, N_EXPERTS), jnp.int32)],
        scratch_shapes=[pltpu.VMEM((1, N_EXPERTS), F32)],
        compiler_params=_params(1),
        name="route",
    )(x, w_router, b_router.reshape(DEPTH, 1, N_EXPERTS))


def _meta_kernel(cnt_ref, idx_ref, rank_ref, dest_ref, be_ref, nu_ref, ps_ref):
    idx = idx_ref[...]
    dest = rank_ref[...]
    run = jnp.int32(0)
    ends = []
    for e in range(N_EXPERTS):
        ps_ref[e] = run
        dest = dest + jnp.where(idx == e, run, 0)
        run = run + ((cnt_ref[e] + (MOE_BLOCK - 1)) & -MOE_BLOCK)
        ends.append(run)
    dest_ref[...] = dest
    nu_ref[0] = lax.shift_right_logical(run, jnp.int32(MOE_BLOCK.bit_length() - 1))

    def body(b, carry):
        first_row = b * MOE_BLOCK
        passed = jnp.int32(0)
        for e in range(N_EXPERTS):
            passed = passed + (ends[e] <= first_row).astype(jnp.int32)
        be_ref[b] = jnp.minimum(passed, N_EXPERTS - 1)
        return carry

    lax.fori_loop(0, be_ref.shape[0], body, 0)


def _meta(counts, idx, rank):
    t = idx.shape[0]
    nb = t * TOP_K // MOE_BLOCK + N_EXPERTS
    smem = pl.BlockSpec(memory_space=pltpu.SMEM)
    vmem = pl.BlockSpec(memory_space=pltpu.VMEM)
    return pl.pallas_call(
        _meta_kernel,
        in_specs=[smem, vmem, vmem],
        out_specs=[vmem, smem, smem, smem],
        out_shape=[jax.ShapeDtypeStruct((t, TOP_K), jnp.int32), jax.ShapeDtypeStruct((nb,), jnp.int32),
                   jax.ShapeDtypeStruct((1,), jnp.int32), jax.ShapeDtypeStruct((N_EXPERTS,), jnp.int32)],
        name="moe_meta",
    )(counts.reshape(N_EXPERTS), idx, rank)


def _row_slab(ref, row):
    return ref.at[pl.ds(pl.multiple_of(row * SLAB, SLAB), SLAB)]


def _dispatch_kernel(dest_ref, cnt_ref, ps_ref, nu_ref, xs_ref, xg_ref, zero_ref, sem, zero_sem):
    i = pl.program_id(0)
    tokens = dest_ref.shape[0] // TOP_K

    @pl.when(i == 0)
    def _():
        zero_ref[...] = jnp.zeros_like(zero_ref)
        block_rows = MOE_BLOCK * SLAB
        n_blocks = xg_ref.shape[0] // block_rows

        def tail_copy(b):
            dst = xg_ref.at[pl.ds(pl.multiple_of(b * block_rows, block_rows), block_rows)]
            return pltpu.make_async_copy(zero_ref, dst, zero_sem)

        def start_tail(b, carry):
            tail_copy(b).start()
            return carry

        def wait_tail(b, carry):
            tail_copy(b).wait()
            return carry

        lax.fori_loop(nu_ref[0], n_blocks, start_tail, 0)
        lax.fori_loop(nu_ref[0], n_blocks, wait_tail, 0)

        def zero_copy(e):
            first = ps_ref[e] + (cnt_ref[e] & -MOE_BLOCK)
            dst = xg_ref.at[pl.ds(pl.multiple_of(first * SLAB, MOE_BLOCK * SLAB), MOE_BLOCK * SLAB)]
            return pltpu.make_async_copy(zero_ref, dst, zero_sem)

        for e in range(N_EXPERTS):
            @pl.when((cnt_ref[e] & (MOE_BLOCK - 1)) != 0)
            def _():
                zero_copy(e).start()
        for e in range(N_EXPERTS):
            @pl.when((cnt_ref[e] & (MOE_BLOCK - 1)) != 0)
            def _():
                zero_copy(e).wait()

    def body(t, carry):
        src = _row_slab(xs_ref, t)
        for k in range(TOP_K):
            pltpu.make_async_copy(src, _row_slab(xg_ref, dest_ref[t * TOP_K + k]), sem).start()
        return carry

    lax.fori_loop(0, tokens, body, 0)
    for _ in range(TOP_K):
        pltpu.make_async_copy(xs_ref, xg_ref.at[pl.ds(0, tokens * SLAB)], sem).wait()


def _dispatch(xs, dest_flat, counts, pad_starts, n_used, *, tokens=256):
    a = dest_flat.shape[0]
    rows = (a // MOE_BLOCK + N_EXPERTS) * MOE_BLOCK
    smem = pl.BlockSpec(memory_space=pltpu.SMEM)
    any_space = pl.BlockSpec(memory_space=pl.ANY)
    return pl.pallas_call(
        _dispatch_kernel,
        grid=(a // (tokens * TOP_K),),
        in_specs=[pl.BlockSpec((tokens * TOP_K,), lambda i: (i,), memory_space=pltpu.SMEM), smem, smem, smem,
                  pl.BlockSpec((tokens * SLAB, LANES), lambda i: (i, 0))],
        out_specs=any_space,
        out_shape=jax.ShapeDtypeStruct((rows * SLAB, LANES), jnp.uint32),
        scratch_shapes=[pltpu.VMEM((MOE_BLOCK * SLAB, LANES), jnp.uint32), pltpu.SemaphoreType.DMA,
                        pltpu.SemaphoreType.DMA],
        compiler_params=_params(1),
        name="moe_dispatch",
    )(dest_flat, counts.reshape(N_EXPERTS), pad_starts, n_used, xs)


def _expert_changed(be_ref, b):
    prev = be_ref[jnp.maximum(b - 1, 0)]
    return jnp.logical_or(b == 0, be_ref[b] != prev)


def _unpack_slabs(xp_ref, xb_ref):
    rows = xb_ref.shape[0]
    for j in range(SLAB):
        lo, hi = _unpack_slab_row(xp_ref, 0, rows, j)
        xb_ref[:, 2 * j * LANES:(2 * j + 1) * LANES] = lo.astype(BF16)
        xb_ref[:, (2 * j + 1) * LANES:(2 * j + 2) * LANES] = hi.astype(BF16)


def _with_expert_weights(be_ref, nu_ref, w_hbm, layer, wres_ref, stage_ref, state_ref, sem, compute):
    b = pl.program_id(0)
    n_used = nu_ref[0]
    n_blocks = be_ref.shape[0]
    n_stages, chunk = stage_ref.shape[0], stage_ref.shape[1]
    n_chunks = wres_ref.shape[1] // chunk

    def chunk_copy(expert, c, stage):
        src = w_hbm.at[layer, expert, pl.ds(pl.multiple_of(c * chunk, chunk), chunk), :]
        return pltpu.make_async_copy(src, stage_ref.at[stage], sem.at[stage])

    piece = min(chunk, 256)

    def cast_chunk(stage, slot, c):
        def body(r, carry):
            rows = pl.ds(pl.multiple_of(r * piece, piece), piece)
            dst = pl.ds(pl.multiple_of(c * chunk + r * piece, piece), piece)
            wres_ref[slot, dst, :] = stage_ref[stage, rows, :].astype(BF16)
            return carry

        lax.fori_loop(0, chunk // piece, body, 0)

    def load_now(expert, slot, first_chunk):
        def body(c, carry):
            chunk_copy(expert, c, 0).start()
            chunk_copy(expert, c, 0).wait()
            cast_chunk(0, slot, c)
            return carry

        lax.fori_loop(first_chunk, n_chunks, body, 0)

    @pl.when(b < n_used)
    def _():
        expert = be_ref[b]

        @pl.when(b == 0)
        def _():
            state_ref[0] = 0
            state_ref[1] = 0
            load_now(expert, 0, 0)

        @pl.when(jnp.logical_and(b > 0, _expert_changed(be_ref, b)))
        def _():
            slot = 1 - state_ref[0]
            load_now(expert, slot, state_ref[1])
            state_ref[0] = slot
            state_ref[1] = 0

        slot = state_ref[0]
        done = state_ref[1]
        def same_expert(j):
            return jnp.logical_and(j < n_used, be_ref[jnp.minimum(j, n_blocks - 1)] == expert)

        j, _ = lax.while_loop(lambda c: c[1], lambda c: (c[0] + 1, same_expert(c[0] + 1)), (b + 1, same_expert(b + 1)))
        upcoming = jnp.where(j < n_used, be_ref[jnp.minimum(j, n_blocks - 1)], -1)
        n_issue = jnp.where(upcoming >= 0, jnp.minimum(n_stages, n_chunks - done), 0)
        for s in range(n_stages):
            @pl.when(s < n_issue)
            def _():
                chunk_copy(upcoming, done + s, s).start()

        compute(wres_ref.at[slot])

        for s in range(n_stages):
            @pl.when(s < n_issue)
            def _():
                chunk_copy(upcoming, done + s, s).wait()
                cast_chunk(s, 1 - slot, done + s)
        state_ref[1] = done + n_issue


def _expert_weight_scratch(rows, cols, chunk, stages):
    return [pltpu.VMEM((2, rows, cols), BF16), pltpu.VMEM((stages, chunk, cols), F32),
            pltpu.SMEM((2,), jnp.int32), pltpu.SemaphoreType.DMA((stages,))]


def _gu_kernel(be_ref, nu_ref, xp_ref, w_hbm, bias_ref, h_ref, xb_ref, wres_ref, stage_ref, state_ref, sem, *, layer):
    def compute(w_ref):
        _unpack_slabs(xp_ref, xb_ref)
        gu = jnp.dot(xb_ref[...], w_ref[...], preferred_element_type=F32) + bias_ref[...]
        g = jnp.minimum(gu[:, :D_EXPERT], SWIGLU_LIMIT)
        u = jnp.clip(gu[:, D_EXPERT:], -SWIGLU_LIMIT, SWIGLU_LIMIT)
        h_ref[...] = ((u + 1.0) * g * jax.nn.sigmoid(SWIGLU_ALPHA * g)).astype(BF16)

    _with_expert_weights(be_ref, nu_ref, w_hbm, layer, wres_ref, stage_ref, state_ref, sem, compute)

    @pl.when(pl.program_id(0) >= nu_ref[0])
    def _():
        h_ref[...] = jnp.zeros_like(h_ref)


def _expert_gu(xg, w_gu, b_gu, block_expert, n_used, layer, *, chunk=1024, stages=2):
    p = xg.shape[0] // SLAB
    d = D_MODEL
    nb = p // MOE_BLOCK
    live = lambda b, nu: jnp.minimum(b, nu[0] - 1)
    grid_spec = pltpu.PrefetchScalarGridSpec(
        num_scalar_prefetch=2,
        grid=(nb,),
        in_specs=[pl.BlockSpec((MOE_BLOCK * SLAB, LANES), lambda b, be, nu: (live(b, nu), 0)),
                  pl.BlockSpec(memory_space=pl.ANY),
                  pl.BlockSpec((None, None, 1, 2 * D_EXPERT), lambda b, be, nu: (layer, be[b], 0, 0))],
        out_specs=pl.BlockSpec((MOE_BLOCK, D_EXPERT), lambda b, be, nu: (b, 0)),
        scratch_shapes=[pltpu.VMEM((MOE_BLOCK, d), BF16)] + _expert_weight_scratch(d, 2 * D_EXPERT, chunk, stages),
    )
    return pl.pallas_call(
        functools.partial(_gu_kernel, layer=layer),
        grid_spec=grid_spec,
        out_shape=jax.ShapeDtypeStruct((p, D_EXPERT), BF16),
        compiler_params=_params(1),
        name="expert_gu",
    )(block_expert, n_used, xg, w_gu, b_gu.reshape(DEPTH, N_EXPERTS, 1, 2 * D_EXPERT))


def _down_kernel(be_ref, nu_ref, h_ref, w_hbm, bd_ref, y_ref, wres_ref, stage_ref, state_ref, sem, *, layer):
    def compute(w_ref):
        _pack_slabs(jnp.dot(h_ref[...], w_ref[...], preferred_element_type=F32) + bd_ref[...], y_ref)

    _with_expert_weights(be_ref, nu_ref, w_hbm, layer, wres_ref, stage_ref, state_ref, sem, compute)

    @pl.when(pl.program_id(0) >= nu_ref[0])
    def _():
        y_ref[...] = jnp.zeros_like(y_ref)


def _expert_down(h, w_down, b_down, block_expert, n_used, layer, *, chunk=192, stages=2):
    p, f = h.shape
    nb = p // MOE_BLOCK
    d = w_down.shape[-1]
    live = lambda b, nu: jnp.minimum(b, nu[0] - 1)
    grid_spec = pltpu.PrefetchScalarGridSpec(
        num_scalar_prefetch=2,
        grid=(nb,),
        in_specs=[pl.BlockSpec((MOE_BLOCK, f), lambda b, be, nu: (live(b, nu), 0)),
                  pl.BlockSpec(memory_space=pl.ANY),
                  pl.BlockSpec((None, None, 1, d), lambda b, be, nu: (layer, be[b], 0, 0))],
        out_specs=pl.BlockSpec((MOE_BLOCK * SLAB, LANES), lambda b, be, nu: (b, 0)),
        scratch_shapes=_expert_weight_scratch(f, d, chunk, stages),
    )
    return pl.pallas_call(
        functools.partial(_down_kernel, layer=layer),
        grid_spec=grid_spec,
        out_shape=jax.ShapeDtypeStruct((p * SLAB, LANES), jnp.uint32),
        compiler_params=_params(1),
        name="expert_down",
    )(block_expert, n_used, h, w_down, b_down.reshape(DEPTH, N_EXPERTS, 1, d))


def _combine_ln_kernel(dcur_ref, dnxt_ref, x_ref, gate_ref, g_ref, b_ref, y_ref, o_ref, ob_ref, buf_ref, sem):
    i = pl.program_id(0)
    n = pl.num_programs(0)
    tm = x_ref.shape[0]
    per_tile = tm * TOP_K
    slot = i % 2

    def issue(dest_ref, tile, to_slot):
        base = (tile % (dest_ref.shape[0] // per_tile)) * per_tile

        def body(t, carry):
            for k in range(TOP_K):
                src = _row_slab(y_ref, dest_ref[base + t * TOP_K + k])
                pltpu.make_async_copy(src, _row_slab(buf_ref.at[to_slot], k * tm + t), sem.at[to_slot]).start()
            return carry

        lax.fori_loop(0, tm, body, 0)

    @pl.when(i == 0)
    def _():
        issue(dcur_ref, i, slot)

    @pl.when(i + 1 < n)
    def _():
        issue(dnxt_ref, i + 1, 1 - slot)

    pltpu.make_async_copy(y_ref.at[pl.ds(0, per_tile * SLAB)], buf_ref.at[slot], sem.at[slot]).wait()
    parts = []
    buf = buf_ref.at[slot]
    for j in range(SLAB):
        acc_lo = acc_hi = None
        for k in range(TOP_K):
            lo, hi = _unpack_slab_row(buf, k * tm * SLAB, tm, j)
            gate = gate_ref[:, k:k + 1]
            acc_lo = gate * lo if acc_lo is None else acc_lo + gate * lo
            acc_hi = gate * hi if acc_hi is None else acc_hi + gate * hi
        parts.extend([acc_lo, acc_hi])
    v = DN_ALPHA * x_ref[...] + jnp.concatenate(parts, axis=1)
    y = _ln_rows(v, g_ref[...], b_ref[...])
    o_ref[...] = y
    ob_ref[...] = y.astype(BF16)


def _combine_ln(x, y_slabs, dest_flat, gates, g, b, layer, *, tm=128, dest_block=1024):
    m, d = x.shape
    tiles_per_block = dest_block // (tm * TOP_K)
    last = m * TOP_K // dest_block - 1
    row = pl.BlockSpec((tm, d), lambda i: (i, 0))
    par = pl.BlockSpec((None, 1, d), lambda i: (layer, 0, 0))
    return pl.pallas_call(
        _combine_ln_kernel,
        grid=(m // tm,),
        in_specs=[pl.BlockSpec((dest_block,), lambda i: (i // tiles_per_block,), memory_space=pltpu.SMEM),
                  pl.BlockSpec((dest_block,), lambda i: (jnp.minimum((i + 1) // tiles_per_block, last),),
                               memory_space=pltpu.SMEM),
                  row, pl.BlockSpec((tm, TOP_K), lambda i: (i, 0)), par, par,
                  pl.BlockSpec(memory_space=pl.ANY)],
        out_specs=[row, row],
        out_shape=[jax.ShapeDtypeStruct((m, d), F32), jax.ShapeDtypeStruct((m, d), BF16)],
        scratch_shapes=[pltpu.VMEM((2, tm * TOP_K * SLAB, LANES), jnp.uint32), pltpu.SemaphoreType.DMA((2,))],
        compiler_params=_params(1),
        name="combine_ln",
    )(dest_flat, dest_flat, x, gates, g.reshape(DEPTH, 1, d), b.reshape(DEPTH, 1, d), y_slabs)


def _swap_halves(w):
    half = w.shape[-1] // 2
    return jnp.concatenate([w[..., half:], w[..., :half]], axis=-1)


def kernel(x, mem, positions, w_in, ml_b_i, ml_b_f, ml_norm_g, mla_g_q, mla_w_uq, mla_g_kv, mla_w_ukv, w_out, ln1_g, ln1_b, x_g_mem, x_b_mem, x_w_q, x_w_kv, x_w_o, ln2_g, ln2_b, w_router, b_router, w_gu, b_gu, w_down, b_down, ln3_g, ln3_b):
    batch, seq, d = x.shape
    assert batch == 1
    t = batch * seq
    x = x.reshape(t, d)
    mem2 = mem.reshape(-1, d)
    pos = positions.reshape(t, 1)

    w_in_t = jnp.swapaxes(w_in, 1, 2)
    kr_w = w_in_t[:, KR_OFF:KR_OFF + MLA_ROPE]
    kr_w_swapped = jnp.concatenate([kr_w[:, MLA_ROPE // 2:], kr_w[:, :MLA_ROPE // 2]], axis=1)
    w_aux_t = jnp.concatenate(
        [w_in_t[:, CQ_OFF:KR_OFF], kr_w, kr_w_swapped, w_in_t[:, GATES_OFF:CQ_OFF],
         jnp.zeros((DEPTH, AUX_WIDTH - (Q_LORA + KV_LORA + 2 * MLA_ROPE + 2 * ML_HEADS), d), w_in.dtype)], axis=1)
    uq = mla_w_uq.reshape(DEPTH, Q_LORA, MLA_HEADS, MLA_QK)
    uq_rope = uq[..., MLA_NOPE:]
    w_uq = jnp.concatenate([uq[..., :MLA_NOPE], uq_rope, _swap_halves(uq_rope)], axis=-1)
    w_uq = w_uq.reshape(DEPTH, Q_LORA, MLA_HEADS * MLA_QPAD)
    gate_bias = jnp.concatenate([ml_b_i, ml_b_f, jnp.zeros((DEPTH, LANES - 2 * ML_HEADS), F32)], axis=-1)
    gate_bias = gate_bias.reshape(DEPTH, 1, LANES)
    half = MLA_ROPE // 2
    inv = ROPE_THETA ** (-jnp.arange(half, dtype=F32) / half)
    inv = jnp.tile(inv, 4).reshape(1, LANES)
    sign = jnp.concatenate([jnp.ones((2 * half,), F32), -jnp.ones((half,), F32), jnp.ones((half,), F32)])
    sign = sign.reshape(1, LANES)

    xb = x.astype(BF16)
    for l in range(DEPTH):
        qkvo = _matmul([xb], w_in_t, l, tm=1024, tn=512, n_out=QKVO_WIDTH, w_transposed=True, name="in_proj")
        aux = _matmul([xb], w_aux_t, l, tm=512, tn=AUX_WIDTH // 2, n_out=AUX_WIDTH, out_dtype=F32, w_transposed=True,
                      name="aux_proj")
        cqn, ckvn, krope, cs, gates = _mixer_prep(aux, pos, inv, sign, mla_g_q, mla_g_kv, gate_bias, l)
        y_ml = _mlstm(qkvo, gates, gates.T, ml_norm_g, l)
        q = _matmul([cqn], w_uq, l, tm=1024, tn=512, n_out=MLA_HEADS * MLA_QPAD, epilogue=_q_epilogue,
                    extra=(cs,), extra_specs=(pl.BlockSpec((1024, LANES), lambda j, i: (i, 0)),), name="q_up")
        k, v = _kv_up(ckvn, mla_w_ukv, krope, l)
        y_mla = _mla_attention(q, k, v)
        h = _matmul([y_ml, y_mla], w_out, l, tm=1024, tn=512, n_out=d, name="out_proj")
        x, xb = _res_ln(x, h, ln1_g, ln1_b, l)
        mem_n = _ln_bf16(mem2, x_g_mem, x_b_mem, l)
        kv = _matmul([mem_n], x_w_kv, l, tm=mem_n.shape[0], tn=512, n_out=2 * X_WIDTH, name="mem_kv")
        qx = _matmul([xb], x_w_q, l, tm=1024, tn=512, n_out=X_WIDTH,
                     epilogue=lambda acc: acc * (X_HEAD_DIM ** -0.5), name="xq_proj")
        ox = _xattn(qx, kv)
        h = _matmul([ox], x_w_o, l, tm=1024, tn=512, n_out=d, name="xo_proj")
        x, x_slabs = _res_ln_slab(x, h, ln2_g, ln2_b, l)
        top_idx, gates_k, rank, counts = _route(x, w_router, b_router, l)
        dest, block_expert, n_used, pad_starts = _meta(counts, top_idx, rank)
        dest_flat = dest.reshape(t * TOP_K)
        xg = _dispatch(x_slabs, dest_flat, counts, pad_starts, n_used)
        hmid = _expert_gu(xg, w_gu, b_gu, block_expert, n_used, l)
        y_slabs = _expert_down(hmid, w_down, b_down, block_expert, n_used, l)
        x, xb = _combine_ln(x, y_slabs, dest_flat, gates_k, ln3_g, ln3_b, l)
    return x.reshape(batch, seq, d)
```

```python
import functools

import jax
import jax.numpy as jnp
from jax import lax
from jax.experimental import pallas as pl
from jax.experimental.pallas import tpu as pltpu

D_MODEL = 4096
DEPTH = 2
CHUNK = 64
ML_HEADS = 8
ML_DQK = 128
ML_DV = 256
ML_WIDTH = ML_HEADS * ML_DV
GATE_SOFTCAP = 15.0
MLA_HEADS = 16
MLA_NOPE = 128
MLA_ROPE = 64
MLA_DV = 128
MLA_QK = MLA_NOPE + MLA_ROPE
MLA_WIDTH = MLA_HEADS * MLA_DV
MLA_QPAD = 256
MLA_VPAD = 256
LOG2_E = 1.4426950408889634
Q_LORA = 1024
KV_LORA = 512
ROPE_THETA = 10000.0
X_HEADS = 4
X_HEAD_DIM = 256
X_WIDTH = X_HEADS * X_HEAD_DIM
N_EXPERTS = 32
TOP_K = 4
D_EXPERT = 768
SWIGLU_LIMIT = 7.0
SWIGLU_ALPHA = 1.702
DN_ALPHA = (2.0 * DEPTH) ** 0.25
LN_EPS = 1e-5
RMS_EPS = 1e-6
QKVO_WIDTH = 2 * ML_HEADS * ML_DQK + 2 * ML_WIDTH
GATES_OFF = QKVO_WIDTH
CQ_OFF = GATES_OFF + 2 * ML_HEADS
CKV_OFF = CQ_OFF + Q_LORA
KR_OFF = CKV_OFF + KV_LORA
AUX_WIDTH = 1792

VMEM_LIMIT = 56 * 1024 * 1024
MOE_BLOCK = 256
LANES = 128
SLAB = D_MODEL // 2 // LANES
ML_CHUNK = 128

BF16 = jnp.bfloat16
F32 = jnp.float32


def _params(n_axes, vmem=VMEM_LIMIT):
    return pltpu.CompilerParams(dimension_semantics=("arbitrary",) * n_axes, vmem_limit_bytes=vmem)


def _cast_weight(w_ref, wb_ref, rows=512):
    k = w_ref.shape[0]
    rows = min(rows, k)

    def body(c, carry):
        sl = pl.ds(pl.multiple_of(c * rows, rows), rows)
        wb_ref[sl, :] = w_ref[sl, :].astype(BF16)
        return carry

    lax.fori_loop(0, k // rows, body, 0)


def _cast_weight_transposed(wt_ref, wb_ref, cols=512):
    k = wt_ref.shape[1]
    for c in range(k // cols):
        wb_ref[c * cols:(c + 1) * cols, :] = wt_ref[:, c * cols:(c + 1) * cols].T.astype(BF16)


def _mm_kernel(*refs, n_x, epilogue, w_transposed):
    x_refs = refs[:n_x]
    w_ref = refs[n_x]
    extra = refs[n_x + 1:-2]
    o_ref, wb_ref = refs[-2:]

    @pl.when(pl.program_id(1) == 0)
    def _():
        if w_transposed:
            _cast_weight_transposed(w_ref, wb_ref)
        else:
            _cast_weight(w_ref, wb_ref)

    off = 0
    acc = None
    for x_ref in x_refs:
        kx = x_ref.shape[1]
        part = jnp.dot(x_ref[...], wb_ref[off:off + kx, :], preferred_element_type=F32)
        acc = part if acc is None else acc + part
        off += kx
    if epilogue is not None:
        acc = epilogue(acc, *extra)
    o_ref[...] = acc.astype(o_ref.dtype)


def _matmul(xs, w, layer, *, tm, tn, n_out, col_off=0, out_dtype=BF16, epilogue=None, extra=(), extra_specs=(),
            w_transposed=False, name="mm"):
    m = xs[0].shape[0]
    k = sum(x.shape[1] for x in xs)
    assert w.shape[2 if w_transposed else 1] == k and m % tm == 0 and n_out % tn == 0 and col_off % tn == 0
    cb = col_off // tn
    in_specs = [pl.BlockSpec((tm, x.shape[1]), lambda j, i: (i, 0)) for x in xs]
    if w_transposed:
        in_specs.append(pl.BlockSpec((None, tn, k), lambda j, i: (layer, j + cb, 0)))
    else:
        in_specs.append(pl.BlockSpec((None, k, tn), lambda j, i: (layer, 0, j + cb)))
    in_specs.extend(extra_specs)
    return pl.pallas_call(
        functools.partial(_mm_kernel, n_x=len(xs), epilogue=epilogue, w_transposed=w_transposed),
        grid=(n_out // tn, m // tm),
        in_specs=in_specs,
        out_specs=pl.BlockSpec((tm, tn), lambda j, i: (i, j)),
        out_shape=jax.ShapeDtypeStruct((m, n_out), out_dtype),
        scratch_shapes=[pltpu.VMEM((k, tn), BF16)],
        compiler_params=_params(2),
        name=name,
    )(*xs, w, *extra)


def _ln_rows(v, g, b):
    mu = jnp.mean(v, axis=-1, keepdims=True)
    c = v - mu
    var = jnp.mean(c * c, axis=-1, keepdims=True)
    return c * lax.rsqrt(var + LN_EPS) * g + b


def _res_ln_kernel(x_ref, h_ref, g_ref, b_ref, o_ref, ob_ref):
    v = DN_ALPHA * x_ref[...] + h_ref[...].astype(F32)
    y = _ln_rows(v, g_ref[...], b_ref[...])
    o_ref[...] = y
    ob_ref[...] = y.astype(BF16)


def _res_ln(x, h, g, b, layer, *, tm=256):
    m, d = x.shape
    row = pl.BlockSpec((tm, d), lambda i: (i, 0))
    par = pl.BlockSpec((None, 1, d), lambda i: (layer, 0, 0))
    return pl.pallas_call(
        _res_ln_kernel,
        grid=(m // tm,),
        in_specs=[row, row, par, par],
        out_specs=[row, row],
        out_shape=[jax.ShapeDtypeStruct((m, d), F32), jax.ShapeDtypeStruct((m, d), BF16)],
        compiler_params=_params(1),
        name="res_ln",
    )(x, h, g.reshape(DEPTH, 1, d), b.reshape(DEPTH, 1, d))


def _pack_slabs(y, slab_ref):
    rows = y.shape[0]
    for j in range(SLAB):
        lo = y[:, 2 * j * LANES:(2 * j + 1) * LANES]
        hi = y[:, (2 * j + 1) * LANES:(2 * j + 2) * LANES]
        slab_ref[pl.ds(j, rows, stride=SLAB), :] = pltpu.pack_elementwise([lo, hi], packed_dtype=BF16)


def _unpack_slab_row(slab_ref, first, rows, j):
    u = slab_ref[pl.ds(first + j, rows, stride=SLAB), :]
    lo = pltpu.unpack_elementwise(u, index=0, packed_dtype=BF16, unpacked_dtype=F32)
    hi = pltpu.unpack_elementwise(u, index=1, packed_dtype=BF16, unpacked_dtype=F32)
    return lo, hi


def _res_ln_route_kernel(x_ref, h_ref, g_ref, b_ref, w_ref, br_ref, o_ref, xp_ref, idx_ref, gate_ref, rank_ref,
                         cnt_ref, carry_ref):
    v = DN_ALPHA * x_ref[...] + h_ref[...].astype(F32)
    y = _ln_rows(v, g_ref[...], b_ref[...])
    o_ref[...] = y
    _pack_slabs(y, xp_ref)
    _route_rows(y, w_ref, br_ref, idx_ref, gate_ref, rank_ref, cnt_ref, carry_ref)


def _res_ln_route(x, h, g, b, w_router, b_router, layer, *, tm=256):
    m, d = x.shape
    row = pl.BlockSpec((tm, d), lambda i: (i, 0))
    par = pl.BlockSpec((None, 1, d), lambda i: (layer, 0, 0))
    tok = pl.BlockSpec((tm, TOP_K), lambda i: (i, 0))
    return pl.pallas_call(
        _res_ln_route_kernel,
        grid=(m // tm,),
        in_specs=[row, row, par, par,
                  pl.BlockSpec((None, d, N_EXPERTS), lambda i: (layer, 0, 0)),
                  pl.BlockSpec((None, 1, N_EXPERTS), lambda i: (layer, 0, 0))],
        out_specs=[row, pl.BlockSpec((tm * SLAB, LANES), lambda i: (i, 0)), tok, tok, tok,
                   pl.BlockSpec((1, N_EXPERTS), lambda i: (0, 0))],
        out_shape=[jax.ShapeDtypeStruct((m, d), F32), jax.ShapeDtypeStruct((m * SLAB, LANES), jnp.uint32),
                   jax.ShapeDtypeStruct((m, TOP_K), jnp.int32), jax.ShapeDtypeStruct((m, TOP_K), F32),
                   jax.ShapeDtypeStruct((m, TOP_K), jnp.int32), jax.ShapeDtypeStruct((1, N_EXPERTS), jnp.int32)],
        scratch_shapes=[pltpu.VMEM((1, N_EXPERTS), F32)],
        compiler_params=_params(1),
        name="res_ln_route",
    )(x, h, g.reshape(DEPTH, 1, d), b.reshape(DEPTH, 1, d), w_router, b_router.reshape(DEPTH, 1, N_EXPERTS))


def _ln_kernel(x_ref, g_ref, b_ref, ob_ref):
    ob_ref[...] = _ln_rows(x_ref[...], g_ref[...], b_ref[...]).astype(BF16)


def _ln_bf16(x, g, b, layer, *, tm=256):
    m, d = x.shape
    row = pl.BlockSpec((tm, d), lambda i: (i, 0))
    par = pl.BlockSpec((None, 1, d), lambda i: (layer, 0, 0))
    return pl.pallas_call(
        _ln_kernel,
        grid=(m // tm,),
        in_specs=[row, par, par],
        out_specs=row,
        out_shape=jax.ShapeDtypeStruct((m, d), BF16),
        compiler_params=_params(1),
        name="mem_ln",
    )(x, g.reshape(DEPTH, 1, d), b.reshape(DEPTH, 1, d))


def _prep_kernel(cq_ref, ckv_ref, misc_ref, pos_ref, inv_ref, sign_ref, gq_ref, gkv_ref, bias_ref,
                 cqn_ref, ckvn_ref, krope_ref, cs_ref, gates_ref):
    def rms(v, g):
        return v * lax.rsqrt(jnp.mean(v * v, axis=-1, keepdims=True) + RMS_EPS) * g

    cqn_ref[...] = rms(cq_ref[...], gq_ref[...]).astype(BF16)
    ckvn_ref[...] = rms(ckv_ref[...], gkv_ref[...]).astype(BF16)
    ang = pos_ref[...].astype(F32) * inv_ref[...]
    lane = lax.broadcasted_iota(jnp.int32, ang.shape, 1)
    cs = jnp.where(lane < MLA_ROPE, jnp.cos(ang), jnp.sin(ang) * sign_ref[...])
    cs_ref[...] = cs
    misc = misc_ref[...]
    t = misc[:, :LANES] * cs
    rot = t + pltpu.roll(t, MLA_ROPE, axis=1)
    krope_ref[...] = jnp.where(lane < MLA_ROPE, rot, 0.0).astype(BF16)
    sc = GATE_SOFTCAP * jnp.tanh((misc[:, LANES:] + bias_ref[...]) / GATE_SOFTCAP)
    log_sig = jnp.minimum(sc, 0.0) - jnp.log(1.0 + jnp.exp(-jnp.abs(sc)))
    gates_ref[...] = jnp.where(lane < ML_HEADS, sc, log_sig)


def _mixer_prep(aux, pos, inv, sign, g_q, g_kv, bias, layer, *, tm=512):
    m = aux.shape[0]
    row = lambda w, c: pl.BlockSpec((tm, w), lambda i: (i, c))
    const = lambda w: pl.BlockSpec((1, w), lambda i: (0, 0))
    par = lambda w: pl.BlockSpec((None, 1, w), lambda i: (layer, 0, 0))
    return pl.pallas_call(
        _prep_kernel,
        grid=(m // tm,),
        in_specs=[row(Q_LORA, 0), row(KV_LORA, Q_LORA // KV_LORA), row(2 * LANES, (Q_LORA + KV_LORA) // (2 * LANES)),
                  pl.BlockSpec((tm, 1), lambda i: (i, 0)), const(LANES), const(LANES),
                  par(Q_LORA), par(KV_LORA), par(LANES)],
        out_specs=[row(Q_LORA, 0), row(KV_LORA, 0), row(LANES, 0), row(LANES, 0), row(LANES, 0)],
        out_shape=[jax.ShapeDtypeStruct((m, Q_LORA), BF16), jax.ShapeDtypeStruct((m, KV_LORA), BF16),
                   jax.ShapeDtypeStruct((m, LANES), BF16), jax.ShapeDtypeStruct((m, LANES), F32),
                   jax.ShapeDtypeStruct((m, LANES), F32)],
        compiler_params=_params(1),
        name="mixer_prep",
    )(aux, aux, aux, pos, inv, sign, g_q.reshape(DEPTH, 1, Q_LORA), g_kv.reshape(DEPTH, 1, KV_LORA), bias)


def _q_epilogue(acc, cs_ref):
    scale = MLA_QK ** -0.5 * LOG2_E
    cs = cs_ref[...]
    outs = []
    for h in range(acc.shape[1] // MLA_QPAD):
        base = h * MLA_QPAD
        outs.append(acc[:, base:base + MLA_NOPE] * scale)
        u = acc[:, base + MLA_NOPE:base + MLA_QPAD] * cs
        outs.append((u + pltpu.roll(u, MLA_ROPE, axis=1)) * scale)
    return jnp.concatenate(outs, axis=1)


def _kv_kernel(x_ref, w_ref, krope_ref, k_ref, v_ref, wb_ref):
    @pl.when(pl.program_id(1) == 0)
    def _():
        _cast_weight(w_ref, wb_ref)

    r = jnp.dot(x_ref[...], wb_ref[...], preferred_element_type=F32)
    kr = krope_ref[...]
    per_head = MLA_NOPE + MLA_DV
    for h in range(r.shape[1] // per_head):
        k_ref[:, h * MLA_QPAD:h * MLA_QPAD + MLA_NOPE] = r[:, h * per_head:h * per_head + MLA_NOPE].astype(BF16)
        k_ref[:, h * MLA_QPAD + MLA_NOPE:(h + 1) * MLA_QPAD] = kr
        v_ref[:, h * MLA_VPAD:h * MLA_VPAD + MLA_DV] = r[:, h * per_head + MLA_NOPE:(h + 1) * per_head].astype(BF16)
        v_ref[:, h * MLA_VPAD + MLA_DV:(h + 1) * MLA_VPAD] = jnp.ones((r.shape[0], MLA_VPAD - MLA_DV), BF16)


def _kv_up(ckvn, w_ukv, krope, layer, *, tm=1024, heads=2):
    m = ckvn.shape[0]
    tn = heads * (MLA_NOPE + MLA_DV)
    return pl.pallas_call(
        _kv_kernel,
        grid=(MLA_HEADS // heads, m // tm),
        in_specs=[pl.BlockSpec((tm, KV_LORA), lambda j, i: (i, 0)),
                  pl.BlockSpec((None, KV_LORA, tn), lambda j, i: (layer, 0, j)),
                  pl.BlockSpec((tm, LANES), lambda j, i: (i, 0))],
        out_specs=[pl.BlockSpec((tm, heads * MLA_QPAD), lambda j, i: (i, j)),
                   pl.BlockSpec((tm, heads * MLA_VPAD), lambda j, i: (i, j))],
        out_shape=[jax.ShapeDtypeStruct((m, MLA_HEADS * MLA_QPAD), BF16),
                   jax.ShapeDtypeStruct((m, MLA_HEADS * MLA_VPAD), BF16)],
        scratch_shapes=[pltpu.VMEM((KV_LORA, tn), BF16)],
        compiler_params=_params(2),
        name="kv_up",
    )(ckvn, w_ukv, krope)


def _mlstm_kernel(q_ref, k_ref, v_ref, o_ref, gc_ref, gr_ref, ng_ref, y_ref, c_ref, n_ref, m_ref):
    L = q_ref.shape[0]
    scale = ML_DQK ** -0.5

    @pl.when(pl.program_id(0) == 0)
    def _():
        c_ref[...] = jnp.zeros_like(c_ref)
        n_ref[...] = jnp.zeros_like(n_ref)
        m_ref[...] = jnp.zeros_like(m_ref)

    gc = gc_ref[...]
    gr = gr_ref[...]
    row = lax.broadcasted_iota(jnp.int32, (L, L), 0)
    col = lax.broadcasted_iota(jnp.int32, (L, L), 1)
    causal = col <= row
    lower = causal.astype(F32)
    upper = (row <= col).astype(F32)
    b_col = jnp.dot(lower, gc, preferred_element_type=F32, precision=lax.Precision.HIGHEST)
    b_row = jnp.dot(gr, upper, preferred_element_type=F32, precision=lax.Precision.HIGHEST)

    for h in range(ML_HEADS):
        bc = b_col[:, ML_HEADS + h:ML_HEADS + h + 1]
        br = b_row[ML_HEADS + h:ML_HEADS + h + 1, :]
        lic = gc[:, h:h + 1]
        lir = gr[h:h + 1, :]
        m_prev = m_ref[h]
        d = jnp.where(causal, bc - br + lir, -jnp.inf)
        a_inter = bc + m_prev
        m_t = jnp.maximum(a_inter, jnp.max(d, axis=-1, keepdims=True))
        w_inter = jnp.exp(a_inter - m_t)
        qh = q_ref[:, h * ML_DQK:(h + 1) * ML_DQK]
        kh = k_ref[:, h * ML_DQK:(h + 1) * ML_DQK]
        vh = v_ref[:, h * ML_DV:(h + 1) * ML_DV]
        s = lax.dot_general(qh, kh, (((1,), (1,)), ((), ())), preferred_element_type=F32)
        p = s * scale * jnp.exp(d - m_t)
        c_prev = c_ref[h]
        n_prev = n_ref[h]
        num = w_inter * jnp.dot(qh, c_prev.astype(BF16), preferred_element_type=F32)
        num = num + jnp.dot(p.astype(BF16), vh, preferred_element_type=F32)
        den = w_inter * jnp.sum(qh.astype(F32) * n_prev, axis=-1, keepdims=True) + jnp.sum(p, axis=-1, keepdims=True)
        hout = num / jnp.maximum(jnp.abs(den), jnp.exp(-m_t))
        m_new = m_t[L - 1:L, :]
        b_last = bc[L - 1:L, :]
        w_prev = jnp.exp(b_last + m_prev - m_new)
        ws_col = jnp.exp(b_last - bc + lic - m_new) * scale
        kw = kh.astype(F32) * ws_col
        c_ref[h] = w_prev * c_prev + lax.dot_general(kw.astype(BF16), vh, (((0,), (0,)), ((), ())),
                                                     preferred_element_type=F32)
        n_ref[h] = w_prev * n_prev + jnp.sum(kw, axis=0, keepdims=True)
        m_ref[h] = m_new
        hn = hout * lax.rsqrt(jnp.mean(hout * hout, axis=-1, keepdims=True) + RMS_EPS)
        hn = hn * ng_ref[:, h * ML_DV:(h + 1) * ML_DV]
        gate = jax.nn.sigmoid(o_ref[:, h * ML_DV:(h + 1) * ML_DV].astype(F32))
        y_ref[:, h * ML_DV:(h + 1) * ML_DV] = (hn * gate).astype(BF16)


def _mlstm(qkvo, gates_col, gates_row, norm_g, layer, *, chunk=ML_CHUNK):
    s = qkvo.shape[0]
    hq = ML_HEADS * ML_DQK
    return pl.pallas_call(
        _mlstm_kernel,
        grid=(s // chunk,),
        in_specs=[pl.BlockSpec((chunk, hq), lambda c: (c, 0)),
                  pl.BlockSpec((chunk, hq), lambda c: (c, 1)),
                  pl.BlockSpec((chunk, ML_WIDTH), lambda c: (c, 2 * hq // ML_WIDTH)),
                  pl.BlockSpec((chunk, ML_WIDTH), lambda c: (c, 2 * hq // ML_WIDTH + 1)),
                  pl.BlockSpec((chunk, LANES), lambda c: (c, 0)),
                  pl.BlockSpec((LANES, chunk), lambda c: (0, c)),
                  pl.BlockSpec((None, 1, ML_WIDTH), lambda c: (layer, 0, 0))],
        out_specs=pl.BlockSpec((chunk, ML_WIDTH), lambda c: (c, 0)),
        out_shape=jax.ShapeDtypeStruct((s, ML_WIDTH), BF16),
        scratch_shapes=[pltpu.VMEM((ML_HEADS, ML_DQK, ML_DV), F32), pltpu.VMEM((ML_HEADS, 1, ML_DQK), F32),
                        pltpu.VMEM((ML_HEADS, 1, 1), F32)],
        compiler_params=_params(1),
        name="mlstm",
    )(qkvo, qkvo, qkvo, qkvo, gates_col, gates_row, norm_g.reshape(DEPTH, 1, ML_WIDTH))


def _mla_kernel(qi_ref, ki_ref, q_ref, k_ref, v_ref, o_ref, m_ref, acc_ref, *, tq, tk):
    p_id = pl.program_id(1)
    qi = qi_ref[p_id]
    ki = ki_ref[p_id]

    @pl.when(ki == 0)
    def _():
        m_ref[...] = jnp.full_like(m_ref, -jnp.inf)
        acc_ref[...] = jnp.zeros_like(acc_ref)

    s = lax.dot_general(q_ref[...], k_ref[...], (((1,), (1,)), ((), ())), preferred_element_type=F32)
    last = (ki + 1) * tk >= (qi + 1) * tq

    def update(s):
        m_prev = m_ref[...]
        m_new = jnp.maximum(m_prev, jnp.max(s, axis=-1, keepdims=True))
        alpha = jnp.exp2(m_prev - m_new)
        p = jnp.exp2((s - m_new).astype(BF16))
        acc_ref[...] = alpha * acc_ref[...] + jnp.dot(p, v_ref[...], preferred_element_type=F32)
        m_ref[...] = m_new

    @pl.when(jnp.logical_not(last))
    def _():
        update(s)

    @pl.when(last)
    def _():
        q_chunk = (qi * tq + lax.broadcasted_iota(jnp.int32, (tq, tk), 0)) // CHUNK
        k_chunk = (ki * tk + lax.broadcasted_iota(jnp.int32, (tq, tk), 1)) // CHUNK
        update(jnp.where(k_chunk <= q_chunk, s, -jnp.inf))
        o_ref[...] = (acc_ref[:, :MLA_DV] / acc_ref[:, MLA_DV:]).astype(BF16)


def _mla_attention(q, k, v, *, tq=1024, tk=1024):
    s = q.shape[0]
    assert tq % tk == 0 and tk % CHUNK == 0
    pairs = [(a, b) for a in range(s // tq) for b in range((a + 1) * tq // tk)]
    qi_tab = jnp.asarray([a for a, _ in pairs], jnp.int32)
    ki_tab = jnp.asarray([b for _, b in pairs], jnp.int32)
    grid_spec = pltpu.PrefetchScalarGridSpec(
        num_scalar_prefetch=2,
        grid=(MLA_HEADS, len(pairs)),
        in_specs=[pl.BlockSpec((tq, MLA_QPAD), lambda h, p, qt, kt: (qt[p], h)),
                  pl.BlockSpec((tk, MLA_QPAD), lambda h, p, qt, kt: (kt[p], h)),
                  pl.BlockSpec((tk, MLA_VPAD), lambda h, p, qt, kt: (kt[p], h))],
        out_specs=pl.BlockSpec((tq, MLA_DV), lambda h, p, qt, kt: (qt[p], h)),
        scratch_shapes=[pltpu.VMEM((tq, 1), F32), pltpu.VMEM((tq, MLA_VPAD), F32)],
    )
    return pl.pallas_call(
        functools.partial(_mla_kernel, tq=tq, tk=tk),
        grid_spec=grid_spec,
        out_shape=jax.ShapeDtypeStruct((s, MLA_WIDTH), BF16),
        compiler_params=_params(2),
        name="mla_attn",
    )(qi_tab, ki_tab, q, k, v)


def _xattn_kernel(q_ref, k_ref, v_ref, o_ref):
    for h in range(X_HEADS):
        sl = slice(h * X_HEAD_DIM, (h + 1) * X_HEAD_DIM)
        s = lax.dot_general(q_ref[:, sl], k_ref[:, sl], (((1,), (1,)), ((), ())), preferred_element_type=F32)
        p = jnp.exp(s - jnp.max(s, axis=-1, keepdims=True))
        o = jnp.dot(p.astype(BF16), v_ref[:, sl], preferred_element_type=F32)
        o_ref[:, sl] = (o / jnp.sum(p, axis=-1, keepdims=True)).astype(BF16)


def _xattn(q, kv, *, tm=512):
    s = q.shape[0]
    n_mem = kv.shape[0]
    return pl.pallas_call(
        _xattn_kernel,
        grid=(s // tm,),
        in_specs=[pl.BlockSpec((tm, X_WIDTH), lambda i: (i, 0)),
                  pl.BlockSpec((n_mem, X_WIDTH), lambda i: (0, 0)),
                  pl.BlockSpec((n_mem, X_WIDTH), lambda i: (0, 1))],
        out_specs=pl.BlockSpec((tm, X_WIDTH), lambda i: (i, 0)),
        out_shape=jax.ShapeDtypeStruct((s, X_WIDTH), BF16),
        compiler_params=_params(1),
        name="xattn",
    )(q, kv, kv)


def _columns(cols):
    lane = lax.broadcasted_iota(jnp.int32, (cols[0].shape[0], len(cols)), 1)
    out = jnp.broadcast_to(cols[-1], lane.shape)
    for k in range(len(cols) - 2, -1, -1):
        out = jnp.where(lane == k, cols[k], out)
    return out


def _route_rows(x, w_ref, b_ref, idx_ref, gate_ref, rank_ref, cnt_ref, carry_ref):
    @pl.when(pl.program_id(0) == 0)
    def _():
        carry_ref[...] = jnp.zeros_like(carry_ref)

    logits = jnp.dot(x, w_ref[...], preferred_element_type=F32, precision=lax.Precision.HIGHEST) + b_ref[...]
    tm = logits.shape[0]
    lane = lax.broadcasted_iota(jnp.int32, logits.shape, 1).astype(F32)
    vals, idxs, hots = [], [], []
    rest = logits
    for _ in range(TOP_K):
        mx = jnp.max(rest, axis=-1, keepdims=True)
        ix = jnp.min(jnp.where(rest == mx, lane, float(N_EXPERTS)), axis=-1, keepdims=True)
        hot = lane == ix
        rest = jnp.where(hot, -jnp.inf, rest)
        vals.append(mx)
        idxs.append(ix)
        hots.append(hot)
    exps = [jnp.exp(v - vals[0]) for v in vals]
    total = exps[0]
    for e in exps[1:]:
        total = total + e
    gate_ref[...] = _columns([e / total for e in exps])
    idx_ref[...] = _columns(idxs).astype(jnp.int32)
    picked = hots[0].astype(F32)
    for hot in hots[1:]:
        picked = picked + hot.astype(F32)
    row = lax.broadcasted_iota(jnp.int32, (tm, tm), 0)
    col = lax.broadcasted_iota(jnp.int32, (tm, tm), 1)
    before = jnp.dot((col < row).astype(BF16), picked.astype(BF16), preferred_element_type=F32) + carry_ref[...]
    ranks = [jnp.sum(jnp.where(hot, before, 0.0), axis=-1, keepdims=True) for hot in hots]
    rank_ref[...] = _columns(ranks).astype(jnp.int32)
    carry = carry_ref[...] + jnp.sum(picked, axis=0, keepdims=True)
    carry_ref[...] = carry
    cnt_ref[...] = carry.astype(jnp.int32)


def _meta_kernel(cnt_ref, idx_ref, rank_ref, dest_ref, be_ref, nu_ref, ps_ref):
    idx = idx_ref[...]
    dest = rank_ref[...]
    run = jnp.int32(0)
    ends = []
    for e in range(N_EXPERTS):
        ps_ref[e] = run
        dest = dest + jnp.where(idx == e, run, 0)
        run = run + ((cnt_ref[e] + (MOE_BLOCK - 1)) & -MOE_BLOCK)
        ends.append(run)
    dest_ref[...] = dest
    nu_ref[0] = lax.shift_right_logical(run, jnp.int32(MOE_BLOCK.bit_length() - 1))

    def body(b, carry):
        first_row = b * MOE_BLOCK
        passed = jnp.int32(0)
        for e in range(N_EXPERTS):
            passed = passed + (ends[e] <= first_row).astype(jnp.int32)
        be_ref[b] = jnp.minimum(passed, N_EXPERTS - 1)
        return carry

    lax.fori_loop(0, be_ref.shape[0], body, 0)


def _meta(counts, idx, rank):
    t = idx.shape[0]
    nb = t * TOP_K // MOE_BLOCK + N_EXPERTS
    smem = pl.BlockSpec(memory_space=pltpu.SMEM)
    vmem = pl.BlockSpec(memory_space=pltpu.VMEM)
    return pl.pallas_call(
        _meta_kernel,
        in_specs=[smem, vmem, vmem],
        out_specs=[vmem, smem, smem, smem],
        out_shape=[jax.ShapeDtypeStruct((t, TOP_K), jnp.int32), jax.ShapeDtypeStruct((nb,), jnp.int32),
                   jax.ShapeDtypeStruct((1,), jnp.int32), jax.ShapeDtypeStruct((N_EXPERTS,), jnp.int32)],
        name="moe_meta",
    )(counts.reshape(N_EXPERTS), idx, rank)


def _row_slab(ref, row):
    return ref.at[pl.ds(pl.multiple_of(row * SLAB, SLAB), SLAB)]


def _dispatch_kernel(dest_ref, cnt_ref, ps_ref, nu_ref, xs_ref, xg_ref, zero_ref, sem, zero_sem):
    i = pl.program_id(0)
    tokens = dest_ref.shape[0] // TOP_K

    @pl.when(i == 0)
    def _():
        zero_ref[...] = jnp.zeros_like(zero_ref)
        block_rows = MOE_BLOCK * SLAB
        n_blocks = xg_ref.shape[0] // block_rows

        def tail_copy(b):
            dst = xg_ref.at[pl.ds(pl.multiple_of(b * block_rows, block_rows), block_rows)]
            return pltpu.make_async_copy(zero_ref, dst, zero_sem)

        def start_tail(b, carry):
            tail_copy(b).start()
            return carry

        def wait_tail(b, carry):
            tail_copy(b).wait()
            return carry

        lax.fori_loop(nu_ref[0], n_blocks, start_tail, 0)
        lax.fori_loop(nu_ref[0], n_blocks, wait_tail, 0)

        def zero_copy(e):
            first = ps_ref[e] + (cnt_ref[e] & -MOE_BLOCK)
            dst = xg_ref.at[pl.ds(pl.multiple_of(first * SLAB, MOE_BLOCK * SLAB), MOE_BLOCK * SLAB)]
            return pltpu.make_async_copy(zero_ref, dst, zero_sem)

        for e in range(N_EXPERTS):
            @pl.when((cnt_ref[e] & (MOE_BLOCK - 1)) != 0)
            def _():
                zero_copy(e).start()
        for e in range(N_EXPERTS):
            @pl.when((cnt_ref[e] & (MOE_BLOCK - 1)) != 0)
            def _():
                zero_copy(e).wait()

    def body(t, carry):
        src = _row_slab(xs_ref, t)
        for k in range(TOP_K):
            pltpu.make_async_copy(src, _row_slab(xg_ref, dest_ref[t * TOP_K + k]), sem).start()
        return carry

    lax.fori_loop(0, tokens, body, 0)
    for _ in range(TOP_K):
        pltpu.make_async_copy(xs_ref, xg_ref.at[pl.ds(0, tokens * SLAB)], sem).wait()


def _dispatch(xs, dest_flat, counts, pad_starts, n_used, *, tokens=256):
    a = dest_flat.shape[0]
    rows = (a // MOE_BLOCK + N_EXPERTS) * MOE_BLOCK
    smem = pl.BlockSpec(memory_space=pltpu.SMEM)
    any_space = pl.BlockSpec(memory_space=pl.ANY)
    return pl.pallas_call(
        _dispatch_kernel,
        grid=(a // (tokens * TOP_K),),
        in_specs=[pl.BlockSpec((tokens * TOP_K,), lambda i: (i,), memory_space=pltpu.SMEM), smem, smem, smem,
                  pl.BlockSpec((tokens * SLAB, LANES), lambda i: (i, 0))],
        out_specs=any_space,
        out_shape=jax.ShapeDtypeStruct((rows * SLAB, LANES), jnp.uint32),
        scratch_shapes=[pltpu.VMEM((MOE_BLOCK * SLAB, LANES), jnp.uint32), pltpu.SemaphoreType.DMA,
                        pltpu.SemaphoreType.DMA],
        compiler_params=_params(1),
        name="moe_dispatch",
    )(dest_flat, counts.reshape(N_EXPERTS), pad_starts, n_used, xs)


def _expert_changed(be_ref, b):
    prev = be_ref[jnp.maximum(b - 1, 0)]
    return jnp.logical_or(b == 0, be_ref[b] != prev)


def _unpack_slabs(xp_ref, xb_ref):
    rows = xb_ref.shape[0]
    for j in range(SLAB):
        lo, hi = _unpack_slab_row(xp_ref, 0, rows, j)
        xb_ref[:, 2 * j * LANES:(2 * j + 1) * LANES] = lo.astype(BF16)
        xb_ref[:, (2 * j + 1) * LANES:(2 * j + 2) * LANES] = hi.astype(BF16)


def _with_expert_weights(be_ref, nu_ref, w_hbm, layer, wres_ref, stage_ref, state_ref, sem, compute):
    b = pl.program_id(0)
    n_used = nu_ref[0]
    n_blocks = be_ref.shape[0]
    n_stages, chunk = stage_ref.shape[0], stage_ref.shape[1]
    n_chunks = wres_ref.shape[1] // chunk

    def chunk_copy(expert, c, stage):
        src = w_hbm.at[layer, expert, pl.ds(pl.multiple_of(c * chunk, chunk), chunk), :]
        return pltpu.make_async_copy(src, stage_ref.at[stage], sem.at[stage])

    piece = min(chunk, 256)

    def cast_chunk(stage, slot, c):
        def body(r, carry):
            rows = pl.ds(pl.multiple_of(r * piece, piece), piece)
            dst = pl.ds(pl.multiple_of(c * chunk + r * piece, piece), piece)
            wres_ref[slot, dst, :] = stage_ref[stage, rows, :].astype(BF16)
            return carry

        lax.fori_loop(0, chunk // piece, body, 0)

    def load_now(expert, slot, first_chunk):
        def body(c, carry):
            chunk_copy(expert, c, 0).start()
            chunk_copy(expert, c, 0).wait()
            cast_chunk(0, slot, c)
            return carry

        lax.fori_loop(first_chunk, n_chunks, body, 0)

    @pl.when(b < n_used)
    def _():
        expert = be_ref[b]

        @pl.when(b == 0)
        def _():
            state_ref[0] = 0
            state_ref[1] = 0
            load_now(expert, 0, 0)

        @pl.when(jnp.logical_and(b > 0, _expert_changed(be_ref, b)))
        def _():
            slot = 1 - state_ref[0]
            load_now(expert, slot, state_ref[1])
            state_ref[0] = slot
            state_ref[1] = 0

        slot = state_ref[0]
        done = state_ref[1]
        def same_expert(j):
            return jnp.logical_and(j < n_used, be_ref[jnp.minimum(j, n_blocks - 1)] == expert)

        j, _ = lax.while_loop(lambda c: c[1], lambda c: (c[0] + 1, same_expert(c[0] + 1)), (b + 1, same_expert(b + 1)))
        upcoming = jnp.where(j < n_used, be_ref[jnp.minimum(j, n_blocks - 1)], -1)
        n_issue = jnp.where(upcoming >= 0, jnp.minimum(n_stages, n_chunks - done), 0)
        for s in range(n_stages):
            @pl.when(s < n_issue)
            def _():
                chunk_copy(upcoming, done + s, s).start()

        compute(wres_ref.at[slot])

        for s in range(n_stages):
            @pl.when(s < n_issue)
            def _():
                chunk_copy(upcoming, done + s, s).wait()
                cast_chunk(s, 1 - slot, done + s)
        state_ref[1] = done + n_issue


def _expert_weight_scratch(rows, cols, chunk, stages):
    return [pltpu.VMEM((2, rows, cols), BF16), pltpu.VMEM((stages, chunk, cols), F32),
            pltpu.SMEM((2,), jnp.int32), pltpu.SemaphoreType.DMA((stages,))]


def _gu_kernel(be_ref, nu_ref, xp_ref, w_hbm, bias_ref, h_ref, xb_ref, wres_ref, stage_ref, state_ref, sem, *, layer):
    def compute(w_ref):
        _unpack_slabs(xp_ref, xb_ref)
        gu = jnp.dot(xb_ref[...], w_ref[...], preferred_element_type=F32) + bias_ref[...]
        g = jnp.minimum(gu[:, :D_EXPERT], SWIGLU_LIMIT)
        u = jnp.clip(gu[:, D_EXPERT:], -SWIGLU_LIMIT, SWIGLU_LIMIT)
        h_ref[...] = ((u + 1.0) * g * jax.nn.sigmoid(SWIGLU_ALPHA * g)).astype(BF16)

    _with_expert_weights(be_ref, nu_ref, w_hbm, layer, wres_ref, stage_ref, state_ref, sem, compute)

    @pl.when(pl.program_id(0) >= nu_ref[0])
    def _():
        h_ref[...] = jnp.zeros_like(h_ref)


def _expert_gu(xg, w_gu, b_gu, block_expert, n_used, layer, *, chunk=1024, stages=2):
    p = xg.shape[0] // SLAB
    d = D_MODEL
    nb = p // MOE_BLOCK
    live = lambda b, nu: jnp.minimum(b, nu[0] - 1)
    grid_spec = pltpu.PrefetchScalarGridSpec(
        num_scalar_prefetch=2,
        grid=(nb,),
        in_specs=[pl.BlockSpec((MOE_BLOCK * SLAB, LANES), lambda b, be, nu: (live(b, nu), 0)),
                  pl.BlockSpec(memory_space=pl.ANY),
                  pl.BlockSpec((None, None, 1, 2 * D_EXPERT), lambda b, be, nu: (layer, be[b], 0, 0))],
        out_specs=pl.BlockSpec((MOE_BLOCK, D_EXPERT), lambda b, be, nu: (b, 0)),
        scratch_shapes=[pltpu.VMEM((MOE_BLOCK, d), BF16)] + _expert_weight_scratch(d, 2 * D_EXPERT, chunk, stages),
    )
    return pl.pallas_call(
        functools.partial(_gu_kernel, layer=layer),
        grid_spec=grid_spec,
        out_shape=jax.ShapeDtypeStruct((p, D_EXPERT), BF16),
        compiler_params=_params(1),
        name="expert_gu",
    )(block_expert, n_used, xg, w_gu, b_gu.reshape(DEPTH, N_EXPERTS, 1, 2 * D_EXPERT))


def _down_kernel(be_ref, nu_ref, h_ref, w_ref, bd_ref, y_ref, wb_ref):
    b = pl.program_id(0)

    @pl.when(_expert_changed(be_ref, b))
    def _():
        _cast_weight(w_ref, wb_ref, rows=256)

    @pl.when(b < nu_ref[0])
    def _():
        _pack_slabs(jnp.dot(h_ref[...], wb_ref[...], preferred_element_type=F32) + bd_ref[...], y_ref)

    @pl.when(b >= nu_ref[0])
    def _():
        y_ref[...] = jnp.zeros_like(y_ref)


def _expert_down(h, w_down, b_down, block_expert, n_used, layer):
    p, f = h.shape
    nb = p // MOE_BLOCK
    d = w_down.shape[-1]
    live = lambda b, nu: jnp.minimum(b, nu[0] - 1)
    grid_spec = pltpu.PrefetchScalarGridSpec(
        num_scalar_prefetch=2,
        grid=(nb,),
        in_specs=[pl.BlockSpec((MOE_BLOCK, f), lambda b, be, nu: (live(b, nu), 0)),
                  pl.BlockSpec((None, None, f, d), lambda b, be, nu: (layer, be[b], 0, 0)),
                  pl.BlockSpec((None, None, 1, d), lambda b, be, nu: (layer, be[b], 0, 0))],
        out_specs=pl.BlockSpec((MOE_BLOCK * SLAB, LANES), lambda b, be, nu: (b, 0)),
        scratch_shapes=[pltpu.VMEM((f, d), BF16)],
    )
    return pl.pallas_call(
        _down_kernel,
        grid_spec=grid_spec,
        out_shape=jax.ShapeDtypeStruct((p * SLAB, LANES), jnp.uint32),
        compiler_params=_params(1),
        name="expert_down",
    )(block_expert, n_used, h, w_down, b_down.reshape(DEPTH, N_EXPERTS, 1, d))


def _combine_ln_kernel(dcur_ref, dnxt_ref, x_ref, gate_ref, g_ref, b_ref, y_ref, o_ref, ob_ref, buf_ref, sem):
    i = pl.program_id(0)
    n = pl.num_programs(0)
    tm = x_ref.shape[0]
    per_tile = tm * TOP_K
    slot = i % 2

    def issue(dest_ref, tile, to_slot):
        base = (tile % (dest_ref.shape[0] // per_tile)) * per_tile

        def body(t, carry):
            for k in range(TOP_K):
                src = _row_slab(y_ref, dest_ref[base + t * TOP_K + k])
                pltpu.make_async_copy(src, _row_slab(buf_ref.at[to_slot], k * tm + t), sem.at[to_slot]).start()
            return carry

        lax.fori_loop(0, tm, body, 0)

    @pl.when(i == 0)
    def _():
        issue(dcur_ref, i, slot)

    @pl.when(i + 1 < n)
    def _():
        issue(dnxt_ref, i + 1, 1 - slot)

    pltpu.make_async_copy(y_ref.at[pl.ds(0, per_tile * SLAB)], buf_ref.at[slot], sem.at[slot]).wait()
    parts = []
    buf = buf_ref.at[slot]
    for j in range(SLAB):
        acc_lo = acc_hi = None
        for k in range(TOP_K):
            lo, hi = _unpack_slab_row(buf, k * tm * SLAB, tm, j)
            gate = gate_ref[:, k:k + 1]
            acc_lo = gate * lo if acc_lo is None else acc_lo + gate * lo
            acc_hi = gate * hi if acc_hi is None else acc_hi + gate * hi
        parts.extend([acc_lo, acc_hi])
    v = DN_ALPHA * x_ref[...] + jnp.concatenate(parts, axis=1)
    y = _ln_rows(v, g_ref[...], b_ref[...])
    o_ref[...] = y
    ob_ref[...] = y.astype(BF16)


def _combine_ln(x, y_slabs, dest_flat, gates, g, b, layer, *, tm=128, dest_block=1024):
    m, d = x.shape
    tiles_per_block = dest_block // (tm * TOP_K)
    last = m * TOP_K // dest_block - 1
    row = pl.BlockSpec((tm, d), lambda i: (i, 0))
    par = pl.BlockSpec((None, 1, d), lambda i: (layer, 0, 0))
    return pl.pallas_call(
        _combine_ln_kernel,
        grid=(m // tm,),
        in_specs=[pl.BlockSpec((dest_block,), lambda i: (i // tiles_per_block,), memory_space=pltpu.SMEM),
                  pl.BlockSpec((dest_block,), lambda i: (jnp.minimum((i + 1) // tiles_per_block, last),),
                               memory_space=pltpu.SMEM),
                  row, pl.BlockSpec((tm, TOP_K), lambda i: (i, 0)), par, par,
                  pl.BlockSpec(memory_space=pl.ANY)],
        out_specs=[row, row],
        out_shape=[jax.ShapeDtypeStruct((m, d), F32), jax.ShapeDtypeStruct((m, d), BF16)],
        scratch_shapes=[pltpu.VMEM((2, tm * TOP_K * SLAB, LANES), jnp.uint32), pltpu.SemaphoreType.DMA((2,))],
        compiler_params=_params(1),
        name="combine_ln",
    )(dest_flat, dest_flat, x, gates, g.reshape(DEPTH, 1, d), b.reshape(DEPTH, 1, d), y_slabs)


def _swap_halves(w):
    half = w.shape[-1] // 2
    return jnp.concatenate([w[..., half:], w[..., :half]], axis=-1)


def kernel(x, mem, positions, w_in, ml_b_i, ml_b_f, ml_norm_g, mla_g_q, mla_w_uq, mla_g_kv, mla_w_ukv, w_out, ln1_g, ln1_b, x_g_mem, x_b_mem, x_w_q, x_w_kv, x_w_o, ln2_g, ln2_b, w_router, b_router, w_gu, b_gu, w_down, b_down, ln3_g, ln3_b):
    batch, seq, d = x.shape
    assert batch == 1
    t = batch * seq
    x = x.reshape(t, d)
    mem2 = mem.reshape(-1, d)
    pos = positions.reshape(t, 1)

    w_in_t = jnp.swapaxes(w_in, 1, 2)
    kr_w = w_in_t[:, KR_OFF:KR_OFF + MLA_ROPE]
    kr_w_swapped = jnp.concatenate([kr_w[:, MLA_ROPE // 2:], kr_w[:, :MLA_ROPE // 2]], axis=1)
    w_aux_t = jnp.concatenate(
        [w_in_t[:, CQ_OFF:KR_OFF], kr_w, kr_w_swapped, w_in_t[:, GATES_OFF:CQ_OFF],
         jnp.zeros((DEPTH, AUX_WIDTH - (Q_LORA + KV_LORA + 2 * MLA_ROPE + 2 * ML_HEADS), d), w_in.dtype)], axis=1)
    uq = mla_w_uq.reshape(DEPTH, Q_LORA, MLA_HEADS, MLA_QK)
    uq_rope = uq[..., MLA_NOPE:]
    w_uq = jnp.concatenate([uq[..., :MLA_NOPE], uq_rope, _swap_halves(uq_rope)], axis=-1)
    w_uq = w_uq.reshape(DEPTH, Q_LORA, MLA_HEADS * MLA_QPAD)
    gate_bias = jnp.concatenate([ml_b_i, ml_b_f, jnp.zeros((DEPTH, LANES - 2 * ML_HEADS), F32)], axis=-1)
    gate_bias = gate_bias.reshape(DEPTH, 1, LANES)
    half = MLA_ROPE // 2
    inv = ROPE_THETA ** (-jnp.arange(half, dtype=F32) / half)
    inv = jnp.tile(inv, 4).reshape(1, LANES)
    sign = jnp.concatenate([jnp.ones((2 * half,), F32), -jnp.ones((half,), F32), jnp.ones((half,), F32)])
    sign = sign.reshape(1, LANES)

    xb = x.astype(BF16)
    for l in range(DEPTH):
        qkvo = _matmul([xb], w_in_t, l, tm=1024, tn=512, n_out=QKVO_WIDTH, w_transposed=True, name="in_proj")
        aux = _matmul([xb], w_aux_t, l, tm=512, tn=AUX_WIDTH // 2, n_out=AUX_WIDTH, out_dtype=F32, w_transposed=True,
                      name="aux_proj")
        cqn, ckvn, krope, cs, gates = _mixer_prep(aux, pos, inv, sign, mla_g_q, mla_g_kv, gate_bias, l)
        y_ml = _mlstm(qkvo, gates, gates.T, ml_norm_g, l)
        q = _matmul([cqn], w_uq, l, tm=1024, tn=512, n_out=MLA_HEADS * MLA_QPAD, epilogue=_q_epilogue,
                    extra=(cs,), extra_specs=(pl.BlockSpec((1024, LANES), lambda j, i: (i, 0)),), name="q_up")
        k, v = _kv_up(ckvn, mla_w_ukv, krope, l)
        y_mla = _mla_attention(q, k, v)
        h = _matmul([y_ml, y_mla], w_out, l, tm=1024, tn=512, n_out=d, name="out_proj")
        x, xb = _res_ln(x, h, ln1_g, ln1_b, l)
        mem_n = _ln_bf16(mem2, x_g_mem, x_b_mem, l)
        kv = _matmul([mem_n], x_w_kv, l, tm=mem_n.shape[0], tn=512, n_out=2 * X_WIDTH, name="mem_kv")
        qx = _matmul([xb], x_w_q, l, tm=1024, tn=512, n_out=X_WIDTH,
                     epilogue=lambda acc: acc * (X_HEAD_DIM ** -0.5), name="xq_proj")
        ox = _xattn(qx, kv)
        h = _matmul([ox], x_w_o, l, tm=1024, tn=512, n_out=d, name="xo_proj")
        x, x_slabs, top_idx, gates_k, rank, counts = _res_ln_route(x, h, ln2_g, ln2_b, w_router, b_router, l)
        dest, block_expert, n_used, pad_starts = _meta(counts, top_idx, rank)
        dest_flat = dest.reshape(t * TOP_K)
        xg = _dispatch(x_slabs, dest_flat, counts, pad_starts, n_used)
        hmid = _expert_gu(xg, w_gu, b_gu, block_expert, n_used, l)
        y_slabs = _expert_down(hmid, w_down, b_down, block_expert, n_used, l)
        x, xb = _combine_ln(x, y_slabs, dest_flat, gates_k, ln3_g, ln3_b, l)
    return x.reshape(batch, seq, d)
```

```python
import functools

import jax
import jax.numpy as jnp
from jax import lax
from jax.experimental import pallas as pl
from jax.experimental.pallas import tpu as pltpu

D_MODEL = 4096
DEPTH = 2
CHUNK = 64
ML_HEADS = 8
ML_DQK = 128
ML_DV = 256
ML_WIDTH = ML_HEADS * ML_DV
GATE_SOFTCAP = 15.0
MLA_HEADS = 16
MLA_NOPE = 128
MLA_ROPE = 64
MLA_DV = 128
MLA_QK = MLA_NOPE + MLA_ROPE
MLA_WIDTH = MLA_HEADS * MLA_DV
MLA_QPAD = 256
MLA_VPAD = 256
LOG2_E = 1.4426950408889634
Q_LORA = 1024
KV_LORA = 512
ROPE_THETA = 10000.0
X_HEADS = 4
X_HEAD_DIM = 256
X_WIDTH = X_HEADS * X_HEAD_DIM
N_EXPERTS = 32
TOP_K = 4
D_EXPERT = 768
SWIGLU_LIMIT = 7.0
SWIGLU_ALPHA = 1.702
DN_ALPHA = (2.0 * DEPTH) ** 0.25
LN_EPS = 1e-5
RMS_EPS = 1e-6
QKVO_WIDTH = 2 * ML_HEADS * ML_DQK + 2 * ML_WIDTH
GATES_OFF = QKVO_WIDTH
CQ_OFF = GATES_OFF + 2 * ML_HEADS
CKV_OFF = CQ_OFF + Q_LORA
KR_OFF = CKV_OFF + KV_LORA
AUX_WIDTH = 1792

VMEM_LIMIT = 56 * 1024 * 1024
MOE_BLOCK = 256
LANES = 128
SLAB = D_MODEL // 2 // LANES
ML_CHUNK = 128

BF16 = jnp.bfloat16
F32 = jnp.float32


def _params(n_axes, vmem=VMEM_LIMIT):
    return pltpu.CompilerParams(dimension_semantics=("arbitrary",) * n_axes, vmem_limit_bytes=vmem)


def _cast_weight(w_ref, wb_ref, rows=512):
    k = w_ref.shape[0]
    rows = min(rows, k)

    def body(c, carry):
        sl = pl.ds(pl.multiple_of(c * rows, rows), rows)
        wb_ref[sl, :] = w_ref[sl, :].astype(BF16)
        return carry

    lax.fori_loop(0, k // rows, body, 0)


def _cast_weight_transposed(wt_ref, wb_ref, cols=512):
    k = wt_ref.shape[1]
    for c in range(k // cols):
        wb_ref[c * cols:(c + 1) * cols, :] = wt_ref[:, c * cols:(c + 1) * cols].T.astype(BF16)


def _mm_kernel(*refs, n_x, epilogue, w_transposed):
    x_refs = refs[:n_x]
    w_ref = refs[n_x]
    extra = refs[n_x + 1:-2]
    o_ref, wb_ref = refs[-2:]

    @pl.when(pl.program_id(1) == 0)
    def _():
        if w_transposed:
            _cast_weight_transposed(w_ref, wb_ref)
        else:
            _cast_weight(w_ref, wb_ref)

    off = 0
    acc = None
    for x_ref in x_refs:
        kx = x_ref.shape[1]
        part = jnp.dot(x_ref[...], wb_ref[off:off + kx, :], preferred_element_type=F32)
        acc = part if acc is None else acc + part
        off += kx
    if epilogue is not None:
        acc = epilogue(acc, *extra)
    o_ref[...] = acc.astype(o_ref.dtype)


def _matmul(xs, w, layer, *, tm, tn, n_out, col_off=0, out_dtype=BF16, epilogue=None, extra=(), extra_specs=(),
            w_transposed=False, name="mm"):
    m = xs[0].shape[0]
    k = sum(x.shape[1] for x in xs)
    assert w.shape[2 if w_transposed else 1] == k and m % tm == 0 and n_out % tn == 0 and col_off % tn == 0
    cb = col_off // tn
    in_specs = [pl.BlockSpec((tm, x.shape[1]), lambda j, i: (i, 0)) for x in xs]
    if w_transposed:
        in_specs.append(pl.BlockSpec((None, tn, k), lambda j, i: (layer, j + cb, 0)))
    else:
        in_specs.append(pl.BlockSpec((None, k, tn), lambda j, i: (layer, 0, j + cb)))
    in_specs.extend(extra_specs)
    return pl.pallas_call(
        functools.partial(_mm_kernel, n_x=len(xs), epilogue=epilogue, w_transposed=w_transposed),
        grid=(n_out // tn, m // tm),
        in_specs=in_specs,
        out_specs=pl.BlockSpec((tm, tn), lambda j, i: (i, j)),
        out_shape=jax.ShapeDtypeStruct((m, n_out), out_dtype),
        scratch_shapes=[pltpu.VMEM((k, tn), BF16)],
        compiler_params=_params(2),
        name=name,
    )(*xs, w, *extra)


def _ln_rows(v, g, b):
    mu = jnp.mean(v, axis=-1, keepdims=True)
    c = v - mu
    var = jnp.mean(c * c, axis=-1, keepdims=True)
    return c * lax.rsqrt(var + LN_EPS) * g + b


def _res_ln_kernel(x_ref, h_ref, g_ref, b_ref, o_ref, ob_ref):
    v = DN_ALPHA * x_ref[...] + h_ref[...].astype(F32)
    y = _ln_rows(v, g_ref[...], b_ref[...])
    o_ref[...] = y
    ob_ref[...] = y.astype(BF16)


def _res_ln(x, h, g, b, layer, *, tm=256):
    m, d = x.shape
    row = pl.BlockSpec((tm, d), lambda i: (i, 0))
    par = pl.BlockSpec((None, 1, d), lambda i: (layer, 0, 0))
    return pl.pallas_call(
        _res_ln_kernel,
        grid=(m // tm,),
        in_specs=[row, row, par, par],
        out_specs=[row, row],
        out_shape=[jax.ShapeDtypeStruct((m, d), F32), jax.ShapeDtypeStruct((m, d), BF16)],
        compiler_params=_params(1),
        name="res_ln",
    )(x, h, g.reshape(DEPTH, 1, d), b.reshape(DEPTH, 1, d))


def _pack_slabs(y, slab_ref):
    rows = y.shape[0]
    for j in range(SLAB):
        lo = y[:, 2 * j * LANES:(2 * j + 1) * LANES]
        hi = y[:, (2 * j + 1) * LANES:(2 * j + 2) * LANES]
        slab_ref[pl.ds(j, rows, stride=SLAB), :] = pltpu.pack_elementwise([lo, hi], packed_dtype=BF16)


def _unpack_slab_row(slab_ref, first, rows, j):
    u = slab_ref[pl.ds(first + j, rows, stride=SLAB), :]
    lo = pltpu.unpack_elementwise(u, index=0, packed_dtype=BF16, unpacked_dtype=F32)
    hi = pltpu.unpack_elementwise(u, index=1, packed_dtype=BF16, unpacked_dtype=F32)
    return lo, hi


def _res_ln_route_kernel(x_ref, h_ref, g_ref, b_ref, w_ref, br_ref, o_ref, xp_ref, idx_ref, gate_ref, rank_ref,
                         cnt_ref, carry_ref):
    v = DN_ALPHA * x_ref[...] + h_ref[...].astype(F32)
    y = _ln_rows(v, g_ref[...], b_ref[...])
    o_ref[...] = y
    _pack_slabs(y, xp_ref)
    _route_rows(y, w_ref, br_ref, idx_ref, gate_ref, rank_ref, cnt_ref, carry_ref)


def _res_ln_route(x, h, g, b, w_router, b_router, layer, *, tm=256):
    m, d = x.shape
    row = pl.BlockSpec((tm, d), lambda i: (i, 0))
    par = pl.BlockSpec((None, 1, d), lambda i: (layer, 0, 0))
    tok = pl.BlockSpec((tm, TOP_K), lambda i: (i, 0))
    return pl.pallas_call(
        _res_ln_route_kernel,
        grid=(m // tm,),
        in_specs=[row, row, par, par,
                  pl.BlockSpec((None, d, N_EXPERTS), lambda i: (layer, 0, 0)),
                  pl.BlockSpec((None, 1, N_EXPERTS), lambda i: (layer, 0, 0))],
        out_specs=[row, pl.BlockSpec((tm * SLAB, LANES), lambda i: (i, 0)), tok, tok, tok,
                   pl.BlockSpec((1, N_EXPERTS), lambda i: (0, 0))],
        out_shape=[jax.ShapeDtypeStruct((m, d), F32), jax.ShapeDtypeStruct((m * SLAB, LANES), jnp.uint32),
                   jax.ShapeDtypeStruct((m, TOP_K), jnp.int32), jax.ShapeDtypeStruct((m, TOP_K), F32),
                   jax.ShapeDtypeStruct((m, TOP_K), jnp.int32), jax.ShapeDtypeStruct((1, N_EXPERTS), jnp.int32)],
        scratch_shapes=[pltpu.VMEM((1, N_EXPERTS), F32)],
        compiler_params=_params(1),
        name="res_ln_route",
    )(x, h, g.reshape(DEPTH, 1, d), b.reshape(DEPTH, 1, d), w_router, b_router.reshape(DEPTH, 1, N_EXPERTS))


def _ln_kernel(x_ref, g_ref, b_ref, ob_ref):
    ob_ref[...] = _ln_rows(x_ref[...], g_ref[...], b_ref[...]).astype(BF16)


def _ln_bf16(x, g, b, layer, *, tm=256):
    m, d = x.shape
    row = pl.BlockSpec((tm, d), lambda i: (i, 0))
    par = pl.BlockSpec((None, 1, d), lambda i: (layer, 0, 0))
    return pl.pallas_call(
        _ln_kernel,
        grid=(m // tm,),
        in_specs=[row, par, par],
        out_specs=row,
        out_shape=jax.ShapeDtypeStruct((m, d), BF16),
        compiler_params=_params(1),
        name="mem_ln",
    )(x, g.reshape(DEPTH, 1, d), b.reshape(DEPTH, 1, d))


def _prep_kernel(cq_ref, ckv_ref, misc_ref, pos_ref, inv_ref, sign_ref, gq_ref, gkv_ref, bias_ref,
                 cqn_ref, ckvn_ref, krope_ref, cs_ref, gates_ref):
    def rms(v, g):
        return v * lax.rsqrt(jnp.mean(v * v, axis=-1, keepdims=True) + RMS_EPS) * g

    cqn_ref[...] = rms(cq_ref[...], gq_ref[...]).astype(BF16)
    ckvn_ref[...] = rms(ckv_ref[...], gkv_ref[...]).astype(BF16)
    ang = pos_ref[...].astype(F32) * inv_ref[...]
    lane = lax.broadcasted_iota(jnp.int32, ang.shape, 1)
    cs = jnp.where(lane < MLA_ROPE, jnp.cos(ang), jnp.sin(ang) * sign_ref[...])
    cs_ref[...] = cs
    misc = misc_ref[...]
    t = misc[:, :LANES] * cs
    rot = t + pltpu.roll(t, MLA_ROPE, axis=1)
    krope_ref[...] = jnp.where(lane < MLA_ROPE, rot, 0.0).astype(BF16)
    sc = GATE_SOFTCAP * jnp.tanh((misc[:, LANES:] + bias_ref[...]) / GATE_SOFTCAP)
    log_sig = jnp.minimum(sc, 0.0) - jnp.log(1.0 + jnp.exp(-jnp.abs(sc)))
    gates_ref[...] = jnp.where(lane < ML_HEADS, sc, log_sig)


def _mixer_prep(aux, pos, inv, sign, g_q, g_kv, bias, layer, *, tm=512):
    m = aux.shape[0]
    row = lambda w, c: pl.BlockSpec((tm, w), lambda i: (i, c))
    const = lambda w: pl.BlockSpec((1, w), lambda i: (0, 0))
    par = lambda w: pl.BlockSpec((None, 1, w), lambda i: (layer, 0, 0))
    return pl.pallas_call(
        _prep_kernel,
        grid=(m // tm,),
        in_specs=[row(Q_LORA, 0), row(KV_LORA, Q_LORA // KV_LORA), row(2 * LANES, (Q_LORA + KV_LORA) // (2 * LANES)),
                  pl.BlockSpec((tm, 1), lambda i: (i, 0)), const(LANES), const(LANES),
                  par(Q_LORA), par(KV_LORA), par(LANES)],
        out_specs=[row(Q_LORA, 0), row(KV_LORA, 0), row(LANES, 0), row(LANES, 0), row(LANES, 0)],
        out_shape=[jax.ShapeDtypeStruct((m, Q_LORA), BF16), jax.ShapeDtypeStruct((m, KV_LORA), BF16),
                   jax.ShapeDtypeStruct((m, LANES), BF16), jax.ShapeDtypeStruct((m, LANES), F32),
                   jax.ShapeDtypeStruct((m, LANES), F32)],
        compiler_params=_params(1),
        name="mixer_prep",
    )(aux, aux, aux, pos, inv, sign, g_q.reshape(DEPTH, 1, Q_LORA), g_kv.reshape(DEPTH, 1, KV_LORA), bias)


def _q_epilogue(acc, cs_ref):
    scale = MLA_QK ** -0.5 * LOG2_E
    cs = cs_ref[...]
    outs = []
    for h in range(acc.shape[1] // MLA_QPAD):
        base = h * MLA_QPAD
        outs.append(acc[:, base:base + MLA_NOPE] * scale)
        u = acc[:, base + MLA_NOPE:base + MLA_QPAD] * cs
        outs.append((u + pltpu.roll(u, MLA_ROPE, axis=1)) * scale)
    return jnp.concatenate(outs, axis=1)


def _kv_kernel(x_ref, w_ref, krope_ref, k_ref, v_ref, wb_ref):
    @pl.when(pl.program_id(1) == 0)
    def _():
        _cast_weight(w_ref, wb_ref)

    r = jnp.dot(x_ref[...], wb_ref[...], preferred_element_type=F32)
    kr = krope_ref[...]
    per_head = MLA_NOPE + MLA_DV
    for h in range(r.shape[1] // per_head):
        k_ref[:, h * MLA_QPAD:h * MLA_QPAD + MLA_NOPE] = r[:, h * per_head:h * per_head + MLA_NOPE].astype(BF16)
        k_ref[:, h * MLA_QPAD + MLA_NOPE:(h + 1) * MLA_QPAD] = kr
        v_ref[:, h * MLA_VPAD:h * MLA_VPAD + MLA_DV] = r[:, h * per_head + MLA_NOPE:(h + 1) * per_head].astype(BF16)
        v_ref[:, h * MLA_VPAD + MLA_DV:(h + 1) * MLA_VPAD] = jnp.ones((r.shape[0], MLA_VPAD - MLA_DV), BF16)


def _kv_up(ckvn, w_ukv, krope, layer, *, tm=1024, heads=2):
    m = ckvn.shape[0]
    tn = heads * (MLA_NOPE + MLA_DV)
    return pl.pallas_call(
        _kv_kernel,
        grid=(MLA_HEADS // heads, m // tm),
        in_specs=[pl.BlockSpec((tm, KV_LORA), lambda j, i: (i, 0)),
                  pl.BlockSpec((None, KV_LORA, tn), lambda j, i: (layer, 0, j)),
                  pl.BlockSpec((tm, LANES), lambda j, i: (i, 0))],
        out_specs=[pl.BlockSpec((tm, heads * MLA_QPAD), lambda j, i: (i, j)),
                   pl.BlockSpec((tm, heads * MLA_VPAD), lambda j, i: (i, j))],
        out_shape=[jax.ShapeDtypeStruct((m, MLA_HEADS * MLA_QPAD), BF16),
                   jax.ShapeDtypeStruct((m, MLA_HEADS * MLA_VPAD), BF16)],
        scratch_shapes=[pltpu.VMEM((KV_LORA, tn), BF16)],
        compiler_params=_params(2),
        name="kv_up",
    )(ckvn, w_ukv, krope)


def _mlstm_kernel(q_ref, k_ref, v_ref, o_ref, gc_ref, gr_ref, ng_ref, y_ref, c_ref, n_ref, m_ref):
    L = q_ref.shape[0]
    scale = ML_DQK ** -0.5

    @pl.when(pl.program_id(0) == 0)
    def _():
        c_ref[...] = jnp.zeros_like(c_ref)
        n_ref[...] = jnp.zeros_like(n_ref)
        m_ref[...] = jnp.zeros_like(m_ref)

    gc = gc_ref[...]
    gr = gr_ref[...]
    row = lax.broadcasted_iota(jnp.int32, (L, L), 0)
    col = lax.broadcasted_iota(jnp.int32, (L, L), 1)
    causal = col <= row
    lower = causal.astype(F32)
    upper = (row <= col).astype(F32)
    b_col = jnp.dot(lower, gc, preferred_element_type=F32, precision=lax.Precision.HIGHEST)
    b_row = jnp.dot(gr, upper, preferred_element_type=F32, precision=lax.Precision.HIGHEST)

    for h in range(ML_HEADS):
        bc = b_col[:, ML_HEADS + h:ML_HEADS + h + 1]
        br = b_row[ML_HEADS + h:ML_HEADS + h + 1, :]
        lic = gc[:, h:h + 1]
        lir = gr[h:h + 1, :]
        m_prev = m_ref[h]
        d = jnp.where(causal, bc - br + lir, -jnp.inf)
        a_inter = bc + m_prev
        m_t = jnp.maximum(a_inter, jnp.max(d, axis=-1, keepdims=True))
        w_inter = jnp.exp(a_inter - m_t)
        qh = q_ref[:, h * ML_DQK:(h + 1) * ML_DQK]
        kh = k_ref[:, h * ML_DQK:(h + 1) * ML_DQK]
        vh = v_ref[:, h * ML_DV:(h + 1) * ML_DV]
        s = lax.dot_general(qh, kh, (((1,), (1,)), ((), ())), preferred_element_type=F32)
        p = s * scale * jnp.exp(d - m_t)
        c_prev = c_ref[h]
        n_prev = n_ref[h]
        num = w_inter * jnp.dot(qh, c_prev.astype(BF16), preferred_element_type=F32)
        num = num + jnp.dot(p.astype(BF16), vh, preferred_element_type=F32)
        den = w_inter * jnp.sum(qh.astype(F32) * n_prev, axis=-1, keepdims=True) + jnp.sum(p, axis=-1, keepdims=True)
        hout = num / jnp.maximum(jnp.abs(den), jnp.exp(-m_t))
        m_new = m_t[L - 1:L, :]
        b_last = bc[L - 1:L, :]
        w_prev = jnp.exp(b_last + m_prev - m_new)
        ws_col = jnp.exp(b_last - bc + lic - m_new) * scale
        kw = kh.astype(F32) * ws_col
        c_ref[h] = w_prev * c_prev + lax.dot_general(kw.astype(BF16), vh, (((0,), (0,)), ((), ())),
                                                     preferred_element_type=F32)
        n_ref[h] = w_prev * n_prev + jnp.sum(kw, axis=0, keepdims=True)
        m_ref[h] = m_new
        hn = hout * lax.rsqrt(jnp.mean(hout * hout, axis=-1, keepdims=True) + RMS_EPS)
        hn = hn * ng_ref[:, h * ML_DV:(h + 1) * ML_DV]
        gate = jax.nn.sigmoid(o_ref[:, h * ML_DV:(h + 1) * ML_DV].astype(F32))
        y_ref[:, h * ML_DV:(h + 1) * ML_DV] = (hn * gate).astype(BF16)


def _mlstm(qkvo, gates_col, gates_row, norm_g, layer, *, chunk=ML_CHUNK):
    s = qkvo.shape[0]
    hq = ML_HEADS * ML_DQK
    return pl.pallas_call(
        _mlstm_kernel,
        grid=(s // chunk,),
        in_specs=[pl.BlockSpec((chunk, hq), lambda c: (c, 0)),
                  pl.BlockSpec((chunk, hq), lambda c: (c, 1)),
                  pl.BlockSpec((chunk, ML_WIDTH), lambda c: (c, 2 * hq // ML_WIDTH)),
                  pl.BlockSpec((chunk, ML_WIDTH), lambda c: (c, 2 * hq // ML_WIDTH + 1)),
                  pl.BlockSpec((chunk, LANES), lambda c: (c, 0)),
                  pl.BlockSpec((LANES, chunk), lambda c: (0, c)),
                  pl.BlockSpec((None, 1, ML_WIDTH), lambda c: (layer, 0, 0))],
        out_specs=pl.BlockSpec((chunk, ML_WIDTH), lambda c: (c, 0)),
        out_shape=jax.ShapeDtypeStruct((s, ML_WIDTH), BF16),
        scratch_shapes=[pltpu.VMEM((ML_HEADS, ML_DQK, ML_DV), F32), pltpu.VMEM((ML_HEADS, 1, ML_DQK), F32),
                        pltpu.VMEM((ML_HEADS, 1, 1), F32)],
        compiler_params=_params(1),
        name="mlstm",
    )(qkvo, qkvo, qkvo, qkvo, gates_col, gates_row, norm_g.reshape(DEPTH, 1, ML_WIDTH))


def _mla_kernel(qi_ref, ki_ref, q_ref, k_ref, v_ref, o_ref, m_ref, acc_ref, *, tq, tk):
    p_id = pl.program_id(1)
    qi = qi_ref[p_id]
    ki = ki_ref[p_id]

    @pl.when(ki == 0)
    def _():
        m_ref[...] = jnp.full_like(m_ref, -jnp.inf)
        acc_ref[...] = jnp.zeros_like(acc_ref)

    s = lax.dot_general(q_ref[...], k_ref[...], (((1,), (1,)), ((), ())), preferred_element_type=F32)
    last = (ki + 1) * tk >= (qi + 1) * tq

    def update(s):
        m_prev = m_ref[...]
        m_new = jnp.maximum(m_prev, jnp.max(s, axis=-1, keepdims=True))
        alpha = jnp.exp2(m_prev - m_new)
        p = jnp.exp2((s - m_new).astype(BF16))
        acc_ref[...] = alpha * acc_ref[...] + jnp.dot(p, v_ref[...], preferred_element_type=F32)
        m_ref[...] = m_new

    @pl.when(jnp.logical_not(last))
    def _():
        update(s)

    @pl.when(last)
    def _():
        q_chunk = (qi * tq + lax.broadcasted_iota(jnp.int32, (tq, tk), 0)) // CHUNK
        k_chunk = (ki * tk + lax.broadcasted_iota(jnp.int32, (tq, tk), 1)) // CHUNK
        update(jnp.where(k_chunk <= q_chunk, s, -jnp.inf))
        o_ref[...] = (acc_ref[:, :MLA_DV] / acc_ref[:, MLA_DV:]).astype(BF16)


def _mla_attention(q, k, v, *, tq=1024, tk=1024):
    s = q.shape[0]
    assert tq % tk == 0 and tk % CHUNK == 0
    pairs = [(a, b) for a in range(s // tq) for b in range((a + 1) * tq // tk)]
    qi_tab = jnp.asarray([a for a, _ in pairs], jnp.int32)
    ki_tab = jnp.asarray([b for _, b in pairs], jnp.int32)
    grid_spec = pltpu.PrefetchScalarGridSpec(
        num_scalar_prefetch=2,
        grid=(MLA_HEADS, len(pairs)),
        in_specs=[pl.BlockSpec((tq, MLA_QPAD), lambda h, p, qt, kt: (qt[p], h)),
                  pl.BlockSpec((tk, MLA_QPAD), lambda h, p, qt, kt: (kt[p], h)),
                  pl.BlockSpec((tk, MLA_VPAD), lambda h, p, qt, kt: (kt[p], h))],
        out_specs=pl.BlockSpec((tq, MLA_DV), lambda h, p, qt, kt: (qt[p], h)),
        scratch_shapes=[pltpu.VMEM((tq, 1), F32), pltpu.VMEM((tq, MLA_VPAD), F32)],
    )
    return pl.pallas_call(
        functools.partial(_mla_kernel, tq=tq, tk=tk),
        grid_spec=grid_spec,
        out_shape=jax.ShapeDtypeStruct((s, MLA_WIDTH), BF16),
        compiler_params=_params(2),
        name="mla_attn",
    )(qi_tab, ki_tab, q, k, v)


def _xattn_kernel(q_ref, k_ref, v_ref, o_ref):
    for h in range(X_HEADS):
        sl = slice(h * X_HEAD_DIM, (h + 1) * X_HEAD_DIM)
        s = lax.dot_general(q_ref[:, sl], k_ref[:, sl], (((1,), (1,)), ((), ())), preferred_element_type=F32)
        p = jnp.exp(s - jnp.max(s, axis=-1, keepdims=True))
        o = jnp.dot(p.astype(BF16), v_ref[:, sl], preferred_element_type=F32)
        o_ref[:, sl] = (o / jnp.sum(p, axis=-1, keepdims=True)).astype(BF16)


def _xattn(q, kv, *, tm=512):
    s = q.shape[0]
    n_mem = kv.shape[0]
    return pl.pallas_call(
        _xattn_kernel,
        grid=(s // tm,),
        in_specs=[pl.BlockSpec((tm, X_WIDTH), lambda i: (i, 0)),
                  pl.BlockSpec((n_mem, X_WIDTH), lambda i: (0, 0)),
                  pl.BlockSpec((n_mem, X_WIDTH), lambda i: (0, 1))],
        out_specs=pl.BlockSpec((tm, X_WIDTH), lambda i: (i, 0)),
        out_shape=jax.ShapeDtypeStruct((s, X_WIDTH), BF16),
        compiler_params=_params(1),
        name="xattn",
    )(q, kv, kv)


def _columns(cols):
    lane = lax.broadcasted_iota(jnp.int32, (cols[0].shape[0], len(cols)), 1)
    out = jnp.broadcast_to(cols[-1], lane.shape)
    for k in range(len(cols) - 2, -1, -1):
        out = jnp.where(lane == k, cols[k], out)
    return out


def _route_rows(x, w_ref, b_ref, idx_ref, gate_ref, rank_ref, cnt_ref, carry_ref):
    @pl.when(pl.program_id(0) == 0)
    def _():
        carry_ref[...] = jnp.zeros_like(carry_ref)

    logits = jnp.dot(x, w_ref[...], preferred_element_type=F32, precision=lax.Precision.HIGHEST) + b_ref[...]
    tm = logits.shape[0]
    lane = lax.broadcasted_iota(jnp.int32, logits.shape, 1).astype(F32)
    vals, idxs, hots = [], [], []
    rest = logits
    for _ in range(TOP_K):
        mx = jnp.max(rest, axis=-1, keepdims=True)
        ix = jnp.min(jnp.where(rest == mx, lane, float(N_EXPERTS)), axis=-1, keepdims=True)
        hot = lane == ix
        rest = jnp.where(hot, -jnp.inf, rest)
        vals.append(mx)
        idxs.append(ix)
        hots.append(hot)
    exps = [jnp.exp(v - vals[0]) for v in vals]
    total = exps[0]
    for e in exps[1:]:
        total = total + e
    gate_ref[...] = _columns([e / total for e in exps])
    idx_ref[...] = _columns(idxs).astype(jnp.int32)
    picked = hots[0].astype(F32)
    for hot in hots[1:]:
        picked = picked + hot.astype(F32)
    row = lax.broadcasted_iota(jnp.int32, (tm, tm), 0)
    col = lax.broadcasted_iota(jnp.int32, (tm, tm), 1)
    before = jnp.dot((col < row).astype(BF16), picked.astype(BF16), preferred_element_type=F32) + carry_ref[...]
    ranks = [jnp.sum(jnp.where(hot, before, 0.0), axis=-1, keepdims=True) for hot in hots]
    rank_ref[...] = _columns(ranks).astype(jnp.int32)
    carry = carry_ref[...] + jnp.sum(picked, axis=0, keepdims=True)
    carry_ref[...] = carry
    cnt_ref[...] = carry.astype(jnp.int32)


def _meta_kernel(cnt_ref, idx_ref, rank_ref, dest_ref, be_ref, nu_ref, ps_ref):
    idx = idx_ref[...]
    dest = rank_ref[...]
    run = jnp.int32(0)
    ends = []
    for e in range(N_EXPERTS):
        ps_ref[e] = run
        dest = dest + jnp.where(idx == e, run, 0)
        run = run + ((cnt_ref[e] + (MOE_BLOCK - 1)) & -MOE_BLOCK)
        ends.append(run)
    dest_ref[...] = dest
    nu_ref[0] = lax.shift_right_logical(run, jnp.int32(MOE_BLOCK.bit_length() - 1))

    def body(b, carry):
        first_row = b * MOE_BLOCK
        passed = jnp.int32(0)
        for e in range(N_EXPERTS):
            passed = passed + (ends[e] <= first_row).astype(jnp.int32)
        be_ref[b] = jnp.minimum(passed, N_EXPERTS - 1)
        return carry

    lax.fori_loop(0, be_ref.shape[0], body, 0)


def _meta(counts, idx, rank):
    t = idx.shape[0]
    nb = t * TOP_K // MOE_BLOCK + N_EXPERTS
    smem = pl.BlockSpec(memory_space=pltpu.SMEM)
    vmem = pl.BlockSpec(memory_space=pltpu.VMEM)
    return pl.pallas_call(
        _meta_kernel,
        in_specs=[smem, vmem, vmem],
        out_specs=[vmem, smem, smem, smem],
        out_shape=[jax.ShapeDtypeStruct((t, TOP_K), jnp.int32), jax.ShapeDtypeStruct((nb,), jnp.int32),
                   jax.ShapeDtypeStruct((1,), jnp.int32), jax.ShapeDtypeStruct((N_EXPERTS,), jnp.int32)],
        name="moe_meta",
    )(counts.reshape(N_EXPERTS), idx, rank)


def _row_slab(ref, row):
    return ref.at[pl.ds(pl.multiple_of(row * SLAB, SLAB), SLAB)]


def _dispatch_kernel(dest_ref, cnt_ref, ps_ref, nu_ref, xs_ref, xg_ref, zero_ref, sem, zero_sem):
    i = pl.program_id(0)
    tokens = dest_ref.shape[0] // TOP_K

    @pl.when(i == 0)
    def _():
        zero_ref[...] = jnp.zeros_like(zero_ref)
        block_rows = MOE_BLOCK * SLAB
        n_blocks = xg_ref.shape[0] // block_rows

        def tail_copy(b):
            dst = xg_ref.at[pl.ds(pl.multiple_of(b * block_rows, block_rows), block_rows)]
            return pltpu.make_async_copy(zero_ref, dst, zero_sem)

        def start_tail(b, carry):
            tail_copy(b).start()
            return carry

        def wait_tail(b, carry):
            tail_copy(b).wait()
            return carry

        lax.fori_loop(nu_ref[0], n_blocks, start_tail, 0)
        lax.fori_loop(nu_ref[0], n_blocks, wait_tail, 0)

        def zero_copy(e):
            first = ps_ref[e] + (cnt_ref[e] & -MOE_BLOCK)
            dst = xg_ref.at[pl.ds(pl.multiple_of(first * SLAB, MOE_BLOCK * SLAB), MOE_BLOCK * SLAB)]
            return pltpu.make_async_copy(zero_ref, dst, zero_sem)

        for e in range(N_EXPERTS):
            @pl.when((cnt_ref[e] & (MOE_BLOCK - 1)) != 0)
            def _():
                zero_copy(e).start()
        for e in range(N_EXPERTS):
            @pl.when((cnt_ref[e] & (MOE_BLOCK - 1)) != 0)
            def _():
                zero_copy(e).wait()

    def body(t, carry):
        src = _row_slab(xs_ref, t)
        for k in range(TOP_K):
            pltpu.make_async_copy(src, _row_slab(xg_ref, dest_ref[t * TOP_K + k]), sem).start()
        return carry

    lax.fori_loop(0, tokens, body, 0, unroll=4)
    for _ in range(TOP_K):
        pltpu.make_async_copy(xs_ref, xg_ref.at[pl.ds(0, tokens * SLAB)], sem).wait()


def _dispatch(xs, dest_flat, counts, pad_starts, n_used, *, tokens=256):
    a = dest_flat.shape[0]
    rows = (a // MOE_BLOCK + N_EXPERTS) * MOE_BLOCK
    smem = pl.BlockSpec(memory_space=pltpu.SMEM)
    any_space = pl.BlockSpec(memory_space=pl.ANY)
    return pl.pallas_call(
        _dispatch_kernel,
        grid=(a // (tokens * TOP_K),),
        in_specs=[pl.BlockSpec((tokens * TOP_K,), lambda i: (i,), memory_space=pltpu.SMEM), smem, smem, smem,
                  pl.BlockSpec((tokens * SLAB, LANES), lambda i: (i, 0))],
        out_specs=any_space,
        out_shape=jax.ShapeDtypeStruct((rows * SLAB, LANES), jnp.uint32),
        scratch_shapes=[pltpu.VMEM((MOE_BLOCK * SLAB, LANES), jnp.uint32), pltpu.SemaphoreType.DMA,
                        pltpu.SemaphoreType.DMA],
        compiler_params=_params(1),
        name="moe_dispatch",
    )(dest_flat, counts.reshape(N_EXPERTS), pad_starts, n_used, xs)


def _expert_changed(be_ref, b):
    prev = be_ref[jnp.maximum(b - 1, 0)]
    return jnp.logical_or(b == 0, be_ref[b] != prev)


def _unpack_slabs(xp_ref, xb_ref):
    rows = xb_ref.shape[0]
    for j in range(SLAB):
        lo, hi = _unpack_slab_row(xp_ref, 0, rows, j)
        xb_ref[:, 2 * j * LANES:(2 * j + 1) * LANES] = lo.astype(BF16)
        xb_ref[:, (2 * j + 1) * LANES:(2 * j + 2) * LANES] = hi.astype(BF16)


def _with_expert_weights(be_ref, nu_ref, w_hbm, layer, wres_ref, stage_ref, state_ref, sem, compute):
    b = pl.program_id(0)
    n_used = nu_ref[0]
    n_blocks = be_ref.shape[0]
    n_stages, chunk = stage_ref.shape[0], stage_ref.shape[1]
    n_chunks = wres_ref.shape[1] // chunk

    def chunk_copy(expert, c, stage):
        src = w_hbm.at[layer, expert, pl.ds(pl.multiple_of(c * chunk, chunk), chunk), :]
        return pltpu.make_async_copy(src, stage_ref.at[stage], sem.at[stage])

    piece = min(chunk, 256)

    def cast_chunk(stage, slot, c):
        def body(r, carry):
            rows = pl.ds(pl.multiple_of(r * piece, piece), piece)
            dst = pl.ds(pl.multiple_of(c * chunk + r * piece, piece), piece)
            wres_ref[slot, dst, :] = stage_ref[stage, rows, :].astype(BF16)
            return carry

        lax.fori_loop(0, chunk // piece, body, 0)

    def load_now(expert, slot, first_chunk):
        def body(c, carry):
            chunk_copy(expert, c, 0).start()
            chunk_copy(expert, c, 0).wait()
            cast_chunk(0, slot, c)
            return carry

        lax.fori_loop(first_chunk, n_chunks, body, 0)

    @pl.when(b < n_used)
    def _():
        expert = be_ref[b]

        @pl.when(b == 0)
        def _():
            state_ref[0] = 0
            state_ref[1] = 0
            load_now(expert, 0, 0)

        @pl.when(jnp.logical_and(b > 0, _expert_changed(be_ref, b)))
        def _():
            slot = 1 - state_ref[0]
            load_now(expert, slot, state_ref[1])
            state_ref[0] = slot
            state_ref[1] = 0

        slot = state_ref[0]
        done = state_ref[1]
        def same_expert(j):
            return jnp.logical_and(j < n_used, be_ref[jnp.minimum(j, n_blocks - 1)] == expert)

        j, _ = lax.while_loop(lambda c: c[1], lambda c: (c[0] + 1, same_expert(c[0] + 1)), (b + 1, same_expert(b + 1)))
        upcoming = jnp.where(j < n_used, be_ref[jnp.minimum(j, n_blocks - 1)], -1)
        n_issue = jnp.where(upcoming >= 0, jnp.minimum(n_stages, n_chunks - done), 0)
        for s in range(n_stages):
            @pl.when(s < n_issue)
            def _():
                chunk_copy(upcoming, done + s, s).start()

        compute(wres_ref.at[slot])

        for s in range(n_stages):
            @pl.when(s < n_issue)
            def _():
                chunk_copy(upcoming, done + s, s).wait()
                cast_chunk(s, 1 - slot, done + s)
        state_ref[1] = done + n_issue


def _expert_weight_scratch(rows, cols, chunk, stages):
    return [pltpu.VMEM((2, rows, cols), BF16), pltpu.VMEM((stages, chunk, cols), F32),
            pltpu.SMEM((2,), jnp.int32), pltpu.SemaphoreType.DMA((stages,))]


def _gu_kernel(be_ref, nu_ref, xp_ref, w_hbm, bias_ref, h_ref, xb_ref, wres_ref, stage_ref, state_ref, sem, *, layer):
    def compute(w_ref):
        _unpack_slabs(xp_ref, xb_ref)
        gu = jnp.dot(xb_ref[...], w_ref[...], preferred_element_type=F32) + bias_ref[...]
        g = jnp.minimum(gu[:, :D_EXPERT], SWIGLU_LIMIT)
        u = jnp.clip(gu[:, D_EXPERT:], -SWIGLU_LIMIT, SWIGLU_LIMIT)
        h_ref[...] = ((u + 1.0) * g * jax.nn.sigmoid(SWIGLU_ALPHA * g)).astype(BF16)

    _with_expert_weights(be_ref, nu_ref, w_hbm, layer, wres_ref, stage_ref, state_ref, sem, compute)

    @pl.when(pl.program_id(0) >= nu_ref[0])
    def _():
        h_ref[...] = jnp.zeros_like(h_ref)


def _expert_gu(xg, w_gu, b_gu, block_expert, n_used, layer, *, chunk=1024, stages=1):
    p = xg.shape[0] // SLAB
    d = D_MODEL
    nb = p // MOE_BLOCK
    live = lambda b, nu: jnp.minimum(b, nu[0] - 1)
    grid_spec = pltpu.PrefetchScalarGridSpec(
        num_scalar_prefetch=2,
        grid=(nb,),
        in_specs=[pl.BlockSpec((MOE_BLOCK * SLAB, LANES), lambda b, be, nu: (live(b, nu), 0)),
                  pl.BlockSpec(memory_space=pl.ANY),
                  pl.BlockSpec((None, None, 1, 2 * D_EXPERT), lambda b, be, nu: (layer, be[b], 0, 0))],
        out_specs=pl.BlockSpec((MOE_BLOCK, D_EXPERT), lambda b, be, nu: (b, 0)),
        scratch_shapes=[pltpu.VMEM((MOE_BLOCK, d), BF16)] + _expert_weight_scratch(d, 2 * D_EXPERT, chunk, stages),
    )
    return pl.pallas_call(
        functools.partial(_gu_kernel, layer=layer),
        grid_spec=grid_spec,
        out_shape=jax.ShapeDtypeStruct((p, D_EXPERT), BF16),
        compiler_params=_params(1),
        name="expert_gu",
    )(block_expert, n_used, xg, w_gu, b_gu.reshape(DEPTH, N_EXPERTS, 1, 2 * D_EXPERT))


def _down_kernel(be_ref, nu_ref, h_ref, w_ref, bd_ref, y_ref, wb_ref):
    b = pl.program_id(0)

    @pl.when(_expert_changed(be_ref, b))
    def _():
        _cast_weight(w_ref, wb_ref, rows=256)

    @pl.when(b < nu_ref[0])
    def _():
        _pack_slabs(jnp.dot(h_ref[...], wb_ref[...], preferred_element_type=F32) + bd_ref[...], y_ref)

    @pl.when(b >= nu_ref[0])
    def _():
        y_ref[...] = jnp.zeros_like(y_ref)


def _expert_down(h, w_down, b_down, block_expert, n_used, layer):
    p, f = h.shape
    nb = p // MOE_BLOCK
    d = w_down.shape[-1]
    live = lambda b, nu: jnp.minimum(b, nu[0] - 1)
    grid_spec = pltpu.PrefetchScalarGridSpec(
        num_scalar_prefetch=2,
        grid=(nb,),
        in_specs=[pl.BlockSpec((MOE_BLOCK, f), lambda b, be, nu: (live(b, nu), 0)),
                  pl.BlockSpec((None, None, f, d), lambda b, be, nu: (layer, be[b], 0, 0)),
                  pl.BlockSpec((None, None, 1, d), lambda b, be, nu: (layer, be[b], 0, 0))],
        out_specs=pl.BlockSpec((MOE_BLOCK * SLAB, LANES), lambda b, be, nu: (b, 0)),
        scratch_shapes=[pltpu.VMEM((f, d), BF16)],
    )
    return pl.pallas_call(
        _down_kernel,
        grid_spec=grid_spec,
        out_shape=jax.ShapeDtypeStruct((p * SLAB, LANES), jnp.uint32),
        compiler_params=_params(1),
        name="expert_down",
    )(block_expert, n_used, h, w_down, b_down.reshape(DEPTH, N_EXPERTS, 1, d))


def _combine_ln_kernel(dcur_ref, dnxt_ref, x_ref, gate_ref, g_ref, b_ref, y_ref, o_ref, ob_ref, buf_ref, sem):
    i = pl.program_id(0)
    n = pl.num_programs(0)
    tm = x_ref.shape[0]
    per_tile = tm * TOP_K
    slot = i % 2

    def issue(dest_ref, tile, to_slot):
        base = (tile % (dest_ref.shape[0] // per_tile)) * per_tile

        def body(t, carry):
            for k in range(TOP_K):
                src = _row_slab(y_ref, dest_ref[base + t * TOP_K + k])
                pltpu.make_async_copy(src, _row_slab(buf_ref.at[to_slot], k * tm + t), sem.at[to_slot]).start()
            return carry

        lax.fori_loop(0, tm, body, 0, unroll=4)

    @pl.when(i == 0)
    def _():
        issue(dcur_ref, i, slot)

    @pl.when(i + 1 < n)
    def _():
        issue(dnxt_ref, i + 1, 1 - slot)

    pltpu.make_async_copy(y_ref.at[pl.ds(0, per_tile * SLAB)], buf_ref.at[slot], sem.at[slot]).wait()
    parts = []
    buf = buf_ref.at[slot]
    for j in range(SLAB):
        acc_lo = acc_hi = None
        for k in range(TOP_K):
            lo, hi = _unpack_slab_row(buf, k * tm * SLAB, tm, j)
            gate = gate_ref[:, k:k + 1]
            acc_lo = gate * lo if acc_lo is None else acc_lo + gate * lo
            acc_hi = gate * hi if acc_hi is None else acc_hi + gate * hi
        parts.extend([acc_lo, acc_hi])
    v = DN_ALPHA * x_ref[...] + jnp.concatenate(parts, axis=1)
    y = _ln_rows(v, g_ref[...], b_ref[...])
    o_ref[...] = y
    ob_ref[...] = y.astype(BF16)


def _combine_ln(x, y_slabs, dest_flat, gates, g, b, layer, *, tm=128, dest_block=1024):
    m, d = x.shape
    tiles_per_block = dest_block // (tm * TOP_K)
    last = m * TOP_K // dest_block - 1
    row = pl.BlockSpec((tm, d), lambda i: (i, 0))
    par = pl.BlockSpec((None, 1, d), lambda i: (layer, 0, 0))
    return pl.pallas_call(
        _combine_ln_kernel,
        grid=(m // tm,),
        in_specs=[pl.BlockSpec((dest_block,), lambda i: (i // tiles_per_block,), memory_space=pltpu.SMEM),
                  pl.BlockSpec((dest_block,), lambda i: (jnp.minimum((i + 1) // tiles_per_block, last),),
                               memory_space=pltpu.SMEM),
                  row, pl.BlockSpec((tm, TOP_K), lambda i: (i, 0)), par, par,
                  pl.BlockSpec(memory_space=pl.ANY)],
        out_specs=[row, row],
        out_shape=[jax.ShapeDtypeStruct((m, d), F32), jax.ShapeDtypeStruct((m, d), BF16)],
        scratch_shapes=[pltpu.VMEM((2, tm * TOP_K * SLAB, LANES), jnp.uint32), pltpu.SemaphoreType.DMA((2,))],
        compiler_params=_params(1),
        name="combine_ln",
    )(dest_flat, dest_flat, x, gates, g.reshape(DEPTH, 1, d), b.reshape(DEPTH, 1, d), y_slabs)


def _swap_halves(w):
    half = w.shape[-1] // 2
    return jnp.concatenate([w[..., half:], w[..., :half]], axis=-1)


def kernel(x, mem, positions, w_in, ml_b_i, ml_b_f, ml_norm_g, mla_g_q, mla_w_uq, mla_g_kv, mla_w_ukv, w_out, ln1_g, ln1_b, x_g_mem, x_b_mem, x_w_q, x_w_kv, x_w_o, ln2_g, ln2_b, w_router, b_router, w_gu, b_gu, w_down, b_down, ln3_g, ln3_b):
    batch, seq, d = x.shape
    assert batch == 1
    t = batch * seq
    x = x.reshape(t, d)
    mem2 = mem.reshape(-1, d)
    pos = positions.reshape(t, 1)

    w_in_t = jnp.swapaxes(w_in, 1, 2)
    kr_w = w_in_t[:, KR_OFF:KR_OFF + MLA_ROPE]
    kr_w_swapped = jnp.concatenate([kr_w[:, MLA_ROPE // 2:], kr_w[:, :MLA_ROPE // 2]], axis=1)
    w_aux_t = jnp.concatenate(
        [w_in_t[:, CQ_OFF:KR_OFF], kr_w, kr_w_swapped, w_in_t[:, GATES_OFF:CQ_OFF],
         jnp.zeros((DEPTH, AUX_WIDTH - (Q_LORA + KV_LORA + 2 * MLA_ROPE + 2 * ML_HEADS), d), w_in.dtype)], axis=1)
    uq = mla_w_uq.reshape(DEPTH, Q_LORA, MLA_HEADS, MLA_QK)
    uq_rope = uq[..., MLA_NOPE:]
    w_uq = jnp.concatenate([uq[..., :MLA_NOPE], uq_rope, _swap_halves(uq_rope)], axis=-1)
    w_uq = w_uq.reshape(DEPTH, Q_LORA, MLA_HEADS * MLA_QPAD)
    gate_bias = jnp.concatenate([ml_b_i, ml_b_f, jnp.zeros((DEPTH, LANES - 2 * ML_HEADS), F32)], axis=-1)
    gate_bias = gate_bias.reshape(DEPTH, 1, LANES)
    half = MLA_ROPE // 2
    inv = ROPE_THETA ** (-jnp.arange(half, dtype=F32) / half)
    inv = jnp.tile(inv, 4).reshape(1, LANES)
    sign = jnp.concatenate([jnp.ones((2 * half,), F32), -jnp.ones((half,), F32), jnp.ones((half,), F32)])
    sign = sign.reshape(1, LANES)

    xb = x.astype(BF16)
    for l in range(DEPTH):
        qkvo = _matmul([xb], w_in_t, l, tm=1024, tn=512, n_out=QKVO_WIDTH, w_transposed=True, name="in_proj")
        aux = _matmul([xb], w_aux_t, l, tm=512, tn=AUX_WIDTH // 2, n_out=AUX_WIDTH, out_dtype=F32, w_transposed=True,
                      name="aux_proj")
        cqn, ckvn, krope, cs, gates = _mixer_prep(aux, pos, inv, sign, mla_g_q, mla_g_kv, gate_bias, l)
        y_ml = _mlstm(qkvo, gates, gates.T, ml_norm_g, l)
        q = _matmul([cqn], w_uq, l, tm=1024, tn=512, n_out=MLA_HEADS * MLA_QPAD, epilogue=_q_epilogue,
                    extra=(cs,), extra_specs=(pl.BlockSpec((1024, LANES), lambda j, i: (i, 0)),), name="q_up")
        k, v = _kv_up(ckvn, mla_w_ukv, krope, l)
        y_mla = _mla_attention(q, k, v)
        h = _matmul([y_ml, y_mla], w_out, l, tm=1024, tn=512, n_out=d, name="out_proj")
        x, xb = _res_ln(x, h, ln1_g, ln1_b, l)
        mem_n = _ln_bf16(mem2, x_g_mem, x_b_mem, l)
        kv = _matmul([mem_n], x_w_kv, l, tm=mem_n.shape[0], tn=512, n_out=2 * X_WIDTH, name="mem_kv")
        qx = _matmul([xb], x_w_q, l, tm=1024, tn=512, n_out=X_WIDTH,
                     epilogue=lambda acc: acc * (X_HEAD_DIM ** -0.5), name="xq_proj")
        ox = _xattn(qx, kv)
        h = _matmul([ox], x_w_o, l, tm=1024, tn=512, n_out=d, name="xo_proj")
        x, x_slabs, top_idx, gates_k, rank, counts = _res_ln_route(x, h, ln2_g, ln2_b, w_router, b_router, l)
        dest, block_expert, n_used, pad_starts = _meta(counts, top_idx, rank)
        dest_flat = dest.reshape(t * TOP_K)
        xg = _dispatch(x_slabs, dest_flat, counts, pad_starts, n_used)
        hmid = _expert_gu(xg, w_gu, b_gu, block_expert, n_used, l)
        y_slabs = _expert_down(hmid, w_down, b_down, block_expert, n_used, l)
        x, xb = _combine_ln(x, y_slabs, dest_flat, gates_k, ln3_g, ln3_b, l)
    return x.reshape(batch, seq, d)
```
